```python
import math
import jax, jax.numpy as jnp
from jax import lax
import numpy as np

D_MODEL = 1024
BATCH = 16
SEQ = 2048
DEPTH = 4

N_MIXERS = 3
N_POOL_LAYERS = (DEPTH + 2) // 3
N_CONV_LAYERS = (DEPTH + 1) // 3
N_SSM_LAYERS = DEPTH // 3
POOL_WINDOWS = (2, 4, 8, 16)
POOL_GROUPS = len(POOL_WINDOWS)
POOL_GROUP_DIM = D_MODEL // POOL_GROUPS
CONV_WIDTH = 31
SSM_GROUP_DIM = 16
SSM_GROUPS = D_MODEL // SSM_GROUP_DIM
SSM_STATE = 64
D_FF = -(-8 * D_MODEL // (3 * 256)) * 256
RMS_EPS = 1e-6
LN_EPS = 1e-5
DT_MIN = 1e-3
DT_MAX = 1e-1
LAM_RE_MAX = -1e-4

kernel_name = "hybrid_pool_conv_s5_swiglu"


def rmsnorm(x, g):
    xf = x.astype(jnp.float32)
    y = xf * lax.rsqrt(jnp.mean(xf * xf, axis=-1, keepdims=True) + RMS_EPS) * g.astype(jnp.float32)
    return y.astype(x.dtype)


def pool_mixer(h, w, b, scale):
    bsz, L, _ = h.shape
    hf = h.astype(jnp.float32)
    cs = jnp.cumsum(hf, axis=1)
    t = jnp.arange(L)
    pooled = []
    for g, win in enumerate(POOL_WINDOWS):
        c = cs[..., g * POOL_GROUP_DIM:(g + 1) * POOL_GROUP_DIM]
        prev = jnp.pad(c, ((0, 0), (win, 0), (0, 0)))[:, :L]
        cnt = jnp.minimum(t + 1, win).astype(jnp.float32)[None, :, None]
        pooled.append((c - prev) / cnt)
    diff = (jnp.concatenate(pooled, axis=-1) - hf).astype(h.dtype)
    diff = diff.reshape(bsz, L, POOL_GROUPS, POOL_GROUP_DIM)
    y = jnp.einsum('blgc,gcd->blgd', diff, w).reshape(bsz, L, D_MODEL) + b
    return y * scale


def conv_mixer(h, w_in, b_in, dw, dw_b, ln_g, ln_b, w_out):
    z = h @ w_in + b_in
    a, gate = jnp.split(z, 2, axis=-1)
    u = a * jax.nn.sigmoid(gate)
    u = lax.conv_general_dilated(u, dw[:, None, :], window_strides=(1,), padding=[(CONV_WIDTH - 1, 0)],
                                 dimension_numbers=('NWC', 'WIO', 'NWC'),
                                 feature_group_count=D_MODEL) + dw_b
    uf = u.astype(jnp.float32)
    mu = jnp.mean(uf, axis=-1, keepdims=True)
    var = jnp.mean(jnp.square(uf - mu), axis=-1, keepdims=True)
    un = (uf - mu) * lax.rsqrt(var + LN_EPS) * ln_g.astype(jnp.float32) + ln_b.astype(jnp.float32)
    un = jax.nn.silu(un).astype(h.dtype)
    return un @ w_out


def _ssm_combine(left, right):
    a_i, b_i = left
    a_j, b_j = right
    return a_j * a_i, a_j * b_i + b_j


def ssm_mixer(h, lam_re, lam_im, log_dt, b_re, b_im, c_re, c_im, d, w_a, w_b):
    bsz, L, _ = h.shape
    u = h.astype(jnp.float32)
    ug = u.reshape(bsz, L, SSM_GROUPS, SSM_GROUP_DIM).astype(jnp.complex64)
    lam = lax.complex(jnp.minimum(lam_re.astype(jnp.float32), LAM_RE_MAX), lam_im.astype(jnp.float32))
    dt = jnp.exp(log_dt.astype(jnp.float32))[:, None]
    lam_bar = jnp.exp(lam * dt)
    b_c = lax.complex(b_re.astype(jnp.float32), b_im.astype(jnp.float32))
    b_bar = ((lam_bar - 1.0) / lam)[..., None] * b_c
    bu = jnp.einsum('blgh,gph->blgp', ug, b_bar)
    a = jnp.broadcast_to(lam_bar[None, None], (1, L, SSM_GROUPS, SSM_STATE))
    _, states = lax.associative_scan(_ssm_combine, (a, bu), axis=1)
    c_c = lax.complex(c_re.astype(jnp.float32), c_im.astype(jnp.float32))
    y = jnp.einsum('blgp,ghp->blgh', states, c_c).real.reshape(bsz, L, D_MODEL)
    y = y + d.astype(jnp.float32) * u
    g = jax.nn.gelu(y).astype(h.dtype)
    return (g @ w_a) * jax.nn.sigmoid(g @ w_b)


def swiglu(h, w_gate, w_up, w_down):
    return (jax.nn.silu(h @ w_gate) * (h @ w_up)) @ w_down


def setup_inputs(seed: int = 0) -> dict:
    key = jax.random.key(seed)
    ks = jax.random.split(key, 32)
    f32 = jnp.float32
    nrm = lambda k, shape, s: jax.random.normal(k, shape, f32) * s
    x = jax.random.normal(ks[0], (BATCH, SEQ, D_MODEL), f32)
    mix_norm = 1.0 + nrm(ks[1], (DEPTH, D_MODEL), 0.02)
    ffn_norm = 1.0 + nrm(ks[2], (DEPTH, D_MODEL), 0.02)
    w_gate = nrm(ks[3], (DEPTH, D_MODEL, D_FF), D_MODEL ** -0.5)
    w_up = nrm(ks[4], (DEPTH, D_MODEL, D_FF), D_MODEL ** -0.5)
    w_down = nrm(ks[5], (DEPTH, D_FF, D_MODEL), D_FF ** -0.5)
    pool_w = nrm(ks[6], (N_POOL_LAYERS, POOL_GROUPS, POOL_GROUP_DIM, POOL_GROUP_DIM), POOL_GROUP_DIM ** -0.5)
    pool_b = nrm(ks[7], (N_POOL_LAYERS, D_MODEL), 0.01)
    pool_scale = 1.0 + nrm(ks[8], (N_POOL_LAYERS, D_MODEL), 0.02)
    conv_w_in = nrm(ks[9], (N_CONV_LAYERS, D_MODEL, 2 * D_MODEL), D_MODEL ** -0.5)
    conv_b_in = nrm(ks[10], (N_CONV_LAYERS, 2 * D_MODEL), 0.01)
    conv_dw = nrm(ks[11], (N_CONV_LAYERS, CONV_WIDTH, D_MODEL), CONV_WIDTH ** -0.5)
    conv_dw_b = nrm(ks[12], (N_CONV_LAYERS, D_MODEL), 0.01)
    conv_ln_g = 1.0 + nrm(ks[13], (N_CONV_LAYERS, D_MODEL), 0.02)
    conv_ln_b = nrm(ks[14], (N_CONV_LAYERS, D_MODEL), 0.01)
    conv_w_out = nrm(ks[15], (N_CONV_LAYERS, D_MODEL, D_MODEL), D_MODEL ** -0.5)
    n_idx = jnp.arange(SSM_STATE, dtype=f32)
    ssm_lam_re = -0.5 + nrm(ks[16], (N_SSM_LAYERS, SSM_GROUPS, SSM_STATE), 0.01)
    ssm_lam_im = jnp.broadcast_to(math.pi * n_idx, (N_SSM_LAYERS, SSM_GROUPS, SSM_STATE)) + nrm(ks[17], (N_SSM_LAYERS, SSM_GROUPS, SSM_STATE), 0.01)
    ssm_log_dt = jax.random.uniform(ks[18], (N_SSM_LAYERS, SSM_GROUPS), f32, math.log(DT_MIN), math.log(DT_MAX))
    ssm_b_re = nrm(ks[19], (N_SSM_LAYERS, SSM_GROUPS, SSM_STATE, SSM_GROUP_DIM), (2 * SSM_GROUP_DIM) ** -0.5)
    ssm_b_im = nrm(ks[20], (N_SSM_LAYERS, SSM_GROUPS, SSM_STATE, SSM_GROUP_DIM), (2 * SSM_GROUP_DIM) ** -0.5)
    ssm_c_re = nrm(ks[21], (N_SSM_LAYERS, SSM_GROUPS, SSM_GROUP_DIM, SSM_STATE), SSM_STATE ** -0.5)
    ssm_c_im = nrm(ks[22], (N_SSM_LAYERS, SSM_GROUPS, SSM_GROUP_DIM, SSM_STATE), SSM_STATE ** -0.5)
    ssm_d = nrm(ks[23], (N_SSM_LAYERS, D_MODEL), 1.0)
    ssm_w_glu_a = nrm(ks[24], (N_SSM_LAYERS, D_MODEL, D_MODEL), D_MODEL ** -0.5)
    ssm_w_glu_b = nrm(ks[25], (N_SSM_LAYERS, D_MODEL, D_MODEL), D_MODEL ** -0.5)
    final_norm = 1.0 + nrm(ks[26], (D_MODEL,), 0.02)
    return {"x": x, "mix_norm": mix_norm, "ffn_norm": ffn_norm, "w_gate": w_gate, "w_up": w_up, "w_down": w_down,
            "pool_w": pool_w, "pool_b": pool_b, "pool_scale": pool_scale,
            "conv_w_in": conv_w_in, "conv_b_in": conv_b_in, "conv_dw": conv_dw, "conv_dw_b": conv_dw_b,
            "conv_ln_g": conv_ln_g, "conv_ln_b": conv_ln_b, "conv_w_out": conv_w_out,
            "ssm_lam_re": ssm_lam_re, "ssm_lam_im": ssm_lam_im, "ssm_log_dt": ssm_log_dt,
            "ssm_b_re": ssm_b_re, "ssm_b_im": ssm_b_im, "ssm_c_re": ssm_c_re, "ssm_c_im": ssm_c_im,
            "ssm_d": ssm_d, "ssm_w_glu_a": ssm_w_glu_a, "ssm_w_glu_b": ssm_w_glu_b, "final_norm": final_norm}


def reference(x, mix_norm, ffn_norm, w_gate, w_up, w_down,
              pool_w, pool_b, pool_scale,
              conv_w_in, conv_b_in, conv_dw, conv_dw_b, conv_ln_g, conv_ln_b, conv_w_out,
              ssm_lam_re, ssm_lam_im, ssm_log_dt, ssm_b_re, ssm_b_im, ssm_c_re, ssm_c_im,
              ssm_d, ssm_w_glu_a, ssm_w_glu_b, final_norm):
    h = x
    for i in range(DEPTH):
        kind, j = i % N_MIXERS, i // N_MIXERS
        hn = rmsnorm(h, mix_norm[i])
        if kind == 0:
            mix = pool_mixer(hn, pool_w[j], pool_b[j], pool_scale[j])
        elif kind == 1:
            mix = conv_mixer(hn, conv_w_in[j], conv_b_in[j], conv_dw[j], conv_dw_b[j],
                             conv_ln_g[j], conv_ln_b[j], conv_w_out[j])
        else:
            mix = ssm_mixer(hn, ssm_lam_re[j], ssm_lam_im[j], ssm_log_dt[j], ssm_b_re[j], ssm_b_im[j],
                            ssm_c_re[j], ssm_c_im[j], ssm_d[j], ssm_w_glu_a[j], ssm_w_glu_b[j])
        h = h + mix
        h = h + swiglu(rmsnorm(h, ffn_norm[i]), w_gate[i], w_up[i], w_down[i])
    return rmsnorm(h, final_norm)
```

```python
import functools

import jax
import jax.numpy as jnp
from jax import lax
from jax.experimental import pallas as pl
from jax.experimental.pallas import tpu as pltpu

D_MODEL = 1024
BATCH = 16
SEQ = 2048
DEPTH = 4
N_MIXERS = 3
POOL_WINDOWS = (2, 4, 8, 16)
POOL_GROUP_DIM = D_MODEL // len(POOL_WINDOWS)
CONV_WIDTH = 31
SSM_GROUP_DIM = 16
SSM_GROUPS = D_MODEL // SSM_GROUP_DIM
SSM_STATE = 64
D_FF = 2816
RMS_EPS = 1e-6
LN_EPS = 1e-5
LAM_RE_MAX = -1e-4

ROWS = SEQ * BATCH
TILE_T = 32
TILE_M = TILE_T * BATCH
N_TILES = ROWS // TILE_M
FF_CHUNK = 256
POOL_HALO = 16 * BATCH
CONV_HALO = 32 * BATCH
CONV_ROWS = 32
CONV_LANES = 512
SSM_CHUNK_GROUPS = 16
SSM_CHUNKS = SSM_GROUPS // SSM_CHUNK_GROUPS
SSM_CHUNK_CH = SSM_CHUNK_GROUPS * SSM_GROUP_DIM
SSM_CHUNK_ST = SSM_CHUNK_GROUPS * SSM_STATE
SCAN_LANES = 512
VMEM_LIMIT = 56 * 1024 * 1024

F32 = jnp.float32
BF16 = jnp.bfloat16


def _rmsnorm(x, g):
    return x * lax.rsqrt(jnp.mean(x * x, axis=-1, keepdims=True) + RMS_EPS) * g


def _dot(a, b):
    return jnp.dot(a, b, preferred_element_type=F32)


def _row_spec():
    return pl.BlockSpec((TILE_M, D_MODEL), lambda i: (i, 0))


def _const_spec(shape):
    nd = len(shape)
    return pl.BlockSpec(shape, lambda i: (0,) * nd, pipeline_mode=pl.Buffered(1))


def _call(body, consts, scratch, name):
    return pl.pallas_call(
        body,
        grid=(N_TILES,),
        in_specs=[_row_spec()] + [_const_spec(c.shape) for c in consts],
        out_specs=_row_spec(),
        out_shape=jax.ShapeDtypeStruct((ROWS, D_MODEL), F32),
        scratch_shapes=scratch,
        compiler_params=pltpu.CompilerParams(
            dimension_semantics=("arbitrary",), vmem_limit_bytes=VMEM_LIMIT),
        name=name,
    )


def _ffn_body(h_ref, g_ref, wg_ref, wu_ref, wd_ref, fin_ref, o_ref, act_ref, *, final):
    h = h_ref[...]
    hn = _rmsnorm(h, g_ref[...]).astype(BF16)
    for c in range(D_FF // FF_CHUNK):
        sl = slice(c * FF_CHUNK, (c + 1) * FF_CHUNK)
        gate = _dot(hn, wg_ref[:, sl])
        up = _dot(hn, wu_ref[:, sl])
        act_ref[:, sl] = (gate * jax.nn.sigmoid(gate) * up).astype(BF16)
    out = h + _dot(act_ref[...], wd_ref[...])
    if final:
        out = _rmsnorm(out, fin_ref[...])
    o_ref[...] = out


def _ffn(h, g, wg, wu, wd, fin, final):
    consts = (g, wg, wu, wd, fin)
    call = _call(functools.partial(_ffn_body, final=final), consts,
                 [pltpu.VMEM((TILE_M, D_FF), BF16)], "swiglu_final" if final else "swiglu")
    return call(h, *consts)


def _pool_body(h_ref, g_ref, w_ref, b_ref, s_ref, o_ref, buf_ref):
    i = pl.program_id(0)

    @pl.when(i == 0)
    def _():
        buf_ref[0:POOL_HALO, :] = jnp.zeros((POOL_HALO, D_MODEL), F32)

    h = h_ref[...]
    hn = _rmsnorm(h, g_ref[...])
    buf_ref[POOL_HALO:, :] = hn
    row = lax.broadcasted_iota(jnp.int32, (TILE_M, 1), 0)
    t = i * TILE_T + row // BATCH
    ys = []
    for g, win in enumerate(POOL_WINDOWS):
        cols = slice(g * POOL_GROUP_DIM, (g + 1) * POOL_GROUP_DIM)
        s = buf_ref[:, cols]
        span = 1
        while span < win:
            sh = span * BATCH
            s = s[sh:] + s[:-sh]
            span *= 2
        s = s[s.shape[0] - TILE_M:]
        cnt = jnp.minimum(t + 1, win).astype(F32)
        diff = s / cnt - hn[:, cols]
        ys.append(_dot(diff.astype(BF16), w_ref[g]))
    y = (jnp.concatenate(ys, axis=1) + b_ref[...]) * s_ref[...]
    o_ref[...] = h + y
    buf_ref[0:POOL_HALO, :] = buf_ref[TILE_M:TILE_M + POOL_HALO, :]


def _pool(h, g, w, b, s):
    consts = (g, w, b, s)
    call = _call(_pool_body, consts, [pltpu.VMEM((POOL_HALO + TILE_M, D_MODEL), F32)], "pool_mixer")
    return call(h, *consts)


def _conv_body(h_ref, g_ref, win_ref, bin_ref, dw_ref, dwb_ref, lng_ref, lnb_ref, wout_ref, o_ref,
               u_ref, v_ref):
    i = pl.program_id(0)

    @pl.when(i == 0)
    def _():
        u_ref[0:CONV_HALO, :] = jnp.zeros((CONV_HALO, D_MODEL), F32)

    h = h_ref[...]
    hn = _rmsnorm(h, g_ref[...]).astype(BF16)
    z = _dot(hn, win_ref[...]) + bin_ref[...]
    u_ref[CONV_HALO:, :] = z[:, :D_MODEL] * jax.nn.sigmoid(z[:, D_MODEL:])

    n_rt = CONV_ROWS // 8

    def conv_rows(rc, carry):
        r0 = pl.multiple_of(rc * CONV_ROWS, CONV_ROWS)
        for cb in range(D_MODEL // CONV_LANES):
            cols = slice(cb * CONV_LANES, (cb + 1) * CONV_LANES)
            accs = [dwb_ref[:, cols]] * n_rt
            for k in range(CONV_WIDTH):
                w = dw_ref[k, :, cols]
                base = CONV_HALO - (CONV_WIDTH - 1 - k) * BATCH
                slab = u_ref[pl.ds(r0 + base, CONV_ROWS), cols]
                accs = [accs[rt] + w * slab[rt * 8:(rt + 1) * 8] for rt in range(n_rt)]
            v_ref[pl.ds(r0, CONV_ROWS), cols] = jnp.concatenate(accs, axis=0)
        return carry

    lax.fori_loop(0, TILE_M // CONV_ROWS, conv_rows, 0)

    v = v_ref[...]
    mu = jnp.mean(v, axis=-1, keepdims=True)
    vc = v - mu
    var = jnp.mean(vc * vc, axis=-1, keepdims=True)
    un = vc * lax.rsqrt(var + LN_EPS) * lng_ref[...] + lnb_ref[...]
    un = (un * jax.nn.sigmoid(un)).astype(BF16)
    o_ref[...] = h + _dot(un, wout_ref[...])
    u_ref[0:CONV_HALO, :] = u_ref[TILE_M:TILE_M + CONV_HALO, :]


def _conv(h, g, w_in, b_in, dw8, dwb8, ln_g, ln_b, w_out):
    consts = (g, w_in, b_in, dw8, dwb8, ln_g, ln_b, w_out)
    call = _call(_conv_body, consts,
                 [pltpu.VMEM((CONV_HALO + TILE_M, D_MODEL), F32), pltpu.VMEM((TILE_M, D_MODEL), F32)],
                 "conv_mixer")
    return call(h, *consts)


def _ssm_body(h_ref, g_ref, bmat_ref, cmat_ref, lre_ref, lim_ref, d_ref, wa_ref, wb_ref, o_ref,
              x_ref, st_ref, y_ref):
    i = pl.program_id(0)

    @pl.when(i == 0)
    def _():
        st_ref[...] = jnp.zeros(st_ref.shape, F32)

    h = h_ref[...]
    u = _rmsnorm(h, g_ref[...])
    ub = u.astype(BF16)
    for c in range(SSM_CHUNKS):
        ch = slice(c * SSM_CHUNK_CH, (c + 1) * SSM_CHUNK_CH)
        x_ref[...] = _dot(ub[:, ch], bmat_ref[c])
        for part in range(SSM_CHUNK_ST // SCAN_LANES):
            re = slice(part * SCAN_LANES, (part + 1) * SCAN_LANES)
            im = slice(SSM_CHUNK_ST + part * SCAN_LANES, SSM_CHUNK_ST + (part + 1) * SCAN_LANES)
            lre = jnp.broadcast_to(lre_ref[c, :, re], (BATCH, SCAN_LANES))
            lim = jnp.broadcast_to(lim_ref[c, :, re], (BATCH, SCAN_LANES))

            def step(t, carry, re=re, im=im, lre=lre, lim=lim):
                sre, sim = carry
                r = pl.multiple_of(t * BATCH, BATCH)
                nre = lre * sre - lim * sim + x_ref[pl.ds(r, BATCH), re]
                nim = lre * sim + lim * sre + x_ref[pl.ds(r, BATCH), im]
                x_ref[pl.ds(r, BATCH), re] = nre
                x_ref[pl.ds(r, BATCH), im] = nim
                return nre, nim

            sre, sim = lax.fori_loop(0, TILE_T, step, (st_ref[c, :, re], st_ref[c, :, im]), unroll=4)
            st_ref[c, :, re] = sre
            st_ref[c, :, im] = sim
        y_ref[:, ch] = _dot(x_ref[...].astype(BF16), cmat_ref[c])
    y = y_ref[...] + d_ref[...] * u
    gl = jax.nn.gelu(y, approximate=True).astype(BF16)
    o_ref[...] = h + _dot(gl, wa_ref[...]) * jax.nn.sigmoid(_dot(gl, wb_ref[...]))


def _ssm(h, g, bmat, cmat, lre, lim, d, wa, wb):
    consts = (g, bmat, cmat, lre, lim, d, wa, wb)
    call = _call(_ssm_body, consts,
                 [pltpu.VMEM((TILE_M, 2 * SSM_CHUNK_ST), F32),
                  pltpu.VMEM((SSM_CHUNKS, BATCH, 2 * SSM_CHUNK_ST), F32),
                  pltpu.VMEM((TILE_M, D_MODEL), F32)],
                 "s5_mixer")
    return call(h, *consts)


def _ssm_params(lam_re, lam_im, log_dt, b_re, b_im, c_re, c_im):
    lam = lax.complex(jnp.minimum(lam_re, LAM_RE_MAX), lam_im)
    dt = jnp.exp(log_dt)[:, None]
    lam_bar = jnp.exp(lam * dt)
    b_bar = ((lam_bar - 1.0) / lam)[..., None] * lax.complex(b_re, b_im)
    eye = jnp.eye(SSM_CHUNK_GROUPS, dtype=F32)

    def in_blocks(w):
        w = w.reshape(SSM_CHUNKS, SSM_CHUNK_GROUPS, SSM_STATE, SSM_GROUP_DIM)
        m = jnp.einsum('cgph,gk->cghkp', w, eye)
        return m.reshape(SSM_CHUNKS, SSM_CHUNK_CH, SSM_CHUNK_ST)

    def out_blocks(w):
        w = w.reshape(SSM_CHUNKS, SSM_CHUNK_GROUPS, SSM_GROUP_DIM, SSM_STATE)
        m = jnp.einsum('cghp,gk->cgpkh', w, eye)
        return m.reshape(SSM_CHUNKS, SSM_CHUNK_ST, SSM_CHUNK_CH)

    bmat = jnp.concatenate([in_blocks(b_bar.real), in_blocks(b_bar.imag)], axis=2).astype(BF16)
    cmat = jnp.concatenate([out_blocks(c_re), out_blocks(-c_im)], axis=1).astype(BF16)
    lre = lam_bar.real.reshape(SSM_CHUNKS, 1, SSM_CHUNK_ST)
    lim = lam_bar.imag.reshape(SSM_CHUNKS, 1, SSM_CHUNK_ST)
    return bmat, cmat, lre, lim


def kernel(x, mix_norm, ffn_norm, w_gate, w_up, w_down, pool_w, pool_b, pool_scale, conv_w_in, conv_b_in,
           conv_dw, conv_dw_b, conv_ln_g, conv_ln_b, conv_w_out, ssm_lam_re, ssm_lam_im, ssm_log_dt,
           ssm_b_re, ssm_b_im, ssm_c_re, ssm_c_im, ssm_d, ssm_w_glu_a, ssm_w_glu_b, final_norm):
    assert x.shape == (BATCH, SEQ, D_MODEL) and x.dtype == F32
    row = lambda v: v.reshape(1, -1)
    h = jnp.transpose(x, (1, 0, 2)).reshape(ROWS, D_MODEL)
    for i in range(DEPTH):
        kind, j = i % N_MIXERS, i // N_MIXERS
        g = row(mix_norm[i])
        if kind == 0:
            h = _pool(h, g, pool_w[j].astype(BF16), row(pool_b[j]), row(pool_scale[j]))
        elif kind == 1:
            dw8 = jnp.broadcast_to(conv_dw[j][:, None, :], (CONV_WIDTH, 8, D_MODEL))
            dwb8 = jnp.broadcast_to(row(conv_dw_b[j]), (8, D_MODEL))
            h = _conv(h, g, conv_w_in[j].astype(BF16), row(conv_b_in[j]), dw8, dwb8,
                      row(conv_ln_g[j]), row(conv_ln_b[j]), conv_w_out[j].astype(BF16))
        else:
            bmat, cmat, lre, lim = _ssm_params(ssm_lam_re[j], ssm_lam_im[j], ssm_log_dt[j], ssm_b_re[j],
                                               ssm_b_im[j], ssm_c_re[j], ssm_c_im[j])
            h = _ssm(h, g, bmat, cmat, lre, lim, row(ssm_d[j]),
                     ssm_w_glu_a[j].astype(BF16), ssm_w_glu_b[j].astype(BF16))
        h = _ffn(h, row(ffn_norm[i]), w_gate[i].astype(BF16), w_up[i].astype(BF16),
                 w_down[i].astype(BF16), row(final_norm), final=(i == DEPTH - 1))
    return jnp.transpose(h.reshape(SEQ, BATCH, D_MODEL), (1, 0, 2))
```

```python
import jax
import jax.numpy as jnp
from jax import lax
from jax.experimental import pallas as pl
from jax.experimental.pallas import tpu as pltpu

D_MODEL = 1024
BATCH = 16
SEQ = 2048
DEPTH = 4
N_MIXERS = 3
POOL_WINDOWS = (2, 4, 8, 16)
POOL_GROUP_DIM = D_MODEL // len(POOL_WINDOWS)
CONV_WIDTH = 31
SSM_GROUP_DIM = 16
SSM_GROUPS = D_MODEL // SSM_GROUP_DIM
SSM_STATE = 64
D_FF = 2816
RMS_EPS = 1e-6
LN_EPS = 1e-5
LAM_RE_MAX = -1e-4

LANES = 128
ROWS = SEQ * BATCH
TILE_T = 32
TILE_M = TILE_T * BATCH
N_TILES = ROWS // TILE_M
FF_CHUNK = 256
FF_OUT_CHUNK = 256
POOL_HALO = 16 * BATCH
CONV_HALO = 32 * BATCH
CONV_LANES = 128
CONV_TAPS = 8
CONV_CHAINS = 8
CONV_IN_CHUNK = 256
CONV_LN_ROWS = 128
SSM_CHUNK_GROUPS = 8
SSM_CHUNKS = SSM_GROUPS // SSM_CHUNK_GROUPS
SSM_CHUNK_CH = SSM_CHUNK_GROUPS * SSM_GROUP_DIM
SSM_CHUNK_ST = SSM_CHUNK_GROUPS * SSM_STATE
VMEM_LIMIT = 60 * 1024 * 1024

F32 = jnp.float32
BF16 = jnp.bfloat16


def _rmsnorm(x, g):
    return x * lax.rsqrt(jnp.mean(x * x, axis=-1, keepdims=True) + RMS_EPS) * g


def _dot(a, b):
    return jnp.dot(a, b, preferred_element_type=F32)


def _const_spec(c):
    if c.ndim == 1:
        return pl.BlockSpec(memory_space=pltpu.SMEM)
    nd = c.ndim
    return pl.BlockSpec(c.shape, lambda i: (0,) * nd, pipeline_mode=pl.Buffered(1))


def _interleave(a, b):
    out, ia, ib = [], 0, 0
    while ia < len(a) or ib < len(b):
        if ib >= len(b) or (ia < len(a) and ia * len(b) <= ib * len(a)):
            out.append(a[ia])
            ia += 1
        else:
            out.append(b[ib])
            ib += 1
    return out


def _ffn_stage(mid_ref, g_ref, wg_ref, wu_ref, wd_ref, fin_ref, o_ref, hn_ref, act_ref, *, final):
    def pre():
        m = mid_ref[...]
        o_ref[...] = m
        hn_ref[...] = _rmsnorm(m, g_ref[...]).astype(BF16)

    def up(c):
        sl = slice(c * FF_CHUNK, (c + 1) * FF_CHUNK)
        gate = _dot(hn_ref[...], wg_ref[:, sl])
        val = _dot(hn_ref[...], wu_ref[:, sl])
        act_ref[:, sl] = (gate * jax.nn.sigmoid(gate) * val).astype(BF16)

    def down(n):
        sl = slice(n * FF_OUT_CHUNK, (n + 1) * FF_OUT_CHUNK)
        o_ref[:, sl] = o_ref[:, sl] + _dot(act_ref[...], wd_ref[:, sl])

    def post():
        if final:
            o_ref[...] = _rmsnorm(o_ref[...], fin_ref[...])

    items = [lambda c=c: up(c) for c in range(D_FF // FF_CHUNK)]
    items += [lambda n=n: down(n) for n in range(D_MODEL // FF_OUT_CHUNK)]
    return pre, items, post


def _pool_init(buf_ref):
    buf_ref[0:POOL_HALO, :] = jnp.zeros((POOL_HALO, D_MODEL), F32)


def _pool_stage(h_ref, g_ref, w_ref, b_ref, s_ref, mid_ref, buf_ref):
    def pre():
        buf_ref[POOL_HALO:, :] = _rmsnorm(h_ref[...], g_ref[...])

    def group(g):
        win = POOL_WINDOWS[g]
        cols = slice(g * POOL_GROUP_DIM, (g + 1) * POOL_GROUP_DIM)
        row = lax.broadcasted_iota(jnp.int32, (TILE_M, 1), 0)
        t = pl.program_id(0) * TILE_T + row // BATCH
        s = buf_ref[:, cols]
        span = 1
        while span < win:
            sh = span * BATCH
            s = s[sh:] + s[:-sh]
            span *= 2
        s = s[s.shape[0] - TILE_M:]
        cnt = jnp.minimum(t + 1, win).astype(F32)
        diff = s / cnt - buf_ref[POOL_HALO:, cols]
        y = (_dot(diff.astype(BF16), w_ref[g]) + b_ref[:, cols]) * s_ref[:, cols]
        mid_ref[:, cols] = h_ref[:, cols] + y

    def post():
        buf_ref[0:POOL_HALO, :] = buf_ref[TILE_M:TILE_M + POOL_HALO, :]

    return pre, [lambda g=g: group(g) for g in range(len(POOL_WINDOWS))], post


def _conv_init(hn_ref, u_ref, v_ref, un_ref):
    u_ref[0:CONV_HALO, :] = jnp.zeros((CONV_HALO, D_MODEL), F32)


def _conv_stage(h_ref, never_ref, g_ref, win_ref, bin_ref, dw_ref, dwb_ref, lng_ref, lnb_ref, wout_ref, mid_ref,
                hn_ref, u_ref, v_ref, un_ref):
    never = never_ref[0] != 0
    finished = []

    def start(init):
        if len(finished) < CONV_CHAINS:
            return init
        return jnp.where(never, finished[-CONV_CHAINS], init)

    def pre():
        hn_ref[...] = _rmsnorm(h_ref[...], g_ref[...]).astype(BF16)

    def glu(n):
        a_cols = slice(n * CONV_IN_CHUNK, (n + 1) * CONV_IN_CHUNK)
        g_cols = slice(D_MODEL + n * CONV_IN_CHUNK, D_MODEL + (n + 1) * CONV_IN_CHUNK)
        a = _dot(hn_ref[...], win_ref[:, a_cols]) + bin_ref[:, a_cols]
        gate = _dot(hn_ref[...], win_ref[:, g_cols]) + bin_ref[:, g_cols]
        u_ref[CONV_HALO:, a_cols] = a * jax.nn.sigmoid(gate)

    def conv(cb):
        cols = slice(cb * CONV_LANES, (cb + 1) * CONV_LANES)
        for k0 in range(0, CONV_WIDTH, CONV_TAPS):
            taps = range(k0, min(k0 + CONV_TAPS, CONV_WIDTH))
            ws = {k: dw_ref[k, :, cols] for k in taps}
            loaded = {}

            def src(r):
                if r not in loaded:
                    loaded[r] = u_ref[CONV_HALO + r:CONV_HALO + r + 8, cols]
                return loaded[r]

            for r in range(0, TILE_M, 8):
                acc = start(dwb_ref[:, cols] if k0 == 0 else v_ref[r:r + 8, cols])
                for k in taps:
                    acc = acc + ws[k] * src(r - (CONV_WIDTH - 1 - k) * BATCH)
                v_ref[r:r + 8, cols] = acc
                finished.append(acc)

    def norm(rb):
        rows = slice(rb * CONV_LN_ROWS, (rb + 1) * CONV_LN_ROWS)
        v = v_ref[rows, :]
        mu = jnp.mean(v, axis=-1, keepdims=True)
        vc = v - mu
        var = jnp.mean(vc * vc, axis=-1, keepdims=True)
        un = vc * lax.rsqrt(var + LN_EPS) * lng_ref[...] + lnb_ref[...]
        un_ref[rows, :] = (un * jax.nn.sigmoid(un)).astype(BF16)

    def post():
        mid_ref[...] = h_ref[...] + _dot(un_ref[...], wout_ref[...])
        u_ref[0:CONV_HALO, :] = u_ref[TILE_M:TILE_M + CONV_HALO, :]

    items = [lambda n=n: glu(n) for n in range(D_MODEL // CONV_IN_CHUNK)]
    items += [lambda cb=cb: conv(cb) for cb in range(D_MODEL // CONV_LANES)]
    items += [lambda rb=rb: norm(rb) for rb in range(TILE_M // CONV_LN_ROWS)]
    return pre, items, post


def _ssm_init(u_ref, x_ref, xs_ref, st_ref, gl_ref):
    st_ref[...] = jnp.zeros(st_ref.shape, F32)


def _ssm_stage(h_ref, g_ref, bmat_ref, cmat_ref, lre_ref, lim_ref, d_ref, wa_ref, wb_ref, mid_ref,
               u_ref, x_ref, xs_ref, st_ref, gl_ref):
    re, im = slice(0, SSM_CHUNK_ST), slice(SSM_CHUNK_ST, 2 * SSM_CHUNK_ST)

    def pre():
        u_ref[...] = _rmsnorm(h_ref[...], g_ref[...])

    def drive(c):
        ch = slice(c * SSM_CHUNK_CH, (c + 1) * SSM_CHUNK_CH)
        x_ref[c % 2] = _dot(u_ref[:, ch].astype(BF16), bmat_ref[c])

    def scan(c):
        lre = jnp.broadcast_to(lre_ref[c], (BATCH, SSM_CHUNK_ST))
        lim = jnp.broadcast_to(lim_ref[c], (BATCH, SSM_CHUNK_ST))
        sre, sim = st_ref[c, :, re], st_ref[c, :, im]
        for t in range(TILE_T):
            rows = slice(t * BATCH, (t + 1) * BATCH)
            nre = lre * sre - lim * sim + x_ref[c % 2, rows, re]
            nim = lre * sim + lim * sre + x_ref[c % 2, rows, im]
            xs_ref[c % 2, rows, re] = nre.astype(BF16)
            xs_ref[c % 2, rows, im] = nim.astype(BF16)
            sre, sim = nre, nim
        st_ref[c, :, re] = sre
        st_ref[c, :, im] = sim

    def readout(c):
        ch = slice(c * SSM_CHUNK_CH, (c + 1) * SSM_CHUNK_CH)
        y = _dot(xs_ref[c % 2], cmat_ref[c]) + d_ref[:, ch] * u_ref[:, ch]
        gl_ref[:, ch] = jax.nn.gelu(y, approximate=True).astype(BF16)

    def post():
        gl = gl_ref[...]
        mid_ref[...] = h_ref[...] + _dot(gl, wa_ref[...]) * jax.nn.sigmoid(_dot(gl, wb_ref[...]))

    items = [lambda: drive(0)]
    for c in range(SSM_CHUNKS):
        if c + 1 < SSM_CHUNKS:
            items.append(lambda c=c: drive(c + 1))
        items.append(lambda c=c: scan(c))
        items.append(lambda c=c: readout(c))
    return pre, items, post


def _ssm_params(lam_re, lam_im, log_dt, b_re, b_im, c_re, c_im):
    lr = jnp.minimum(lam_re, LAM_RE_MAX)
    li = lam_im
    dt = jnp.exp(log_dt)[:, None]
    mag = jnp.exp(lr * dt)
    bar_re, bar_im = mag * jnp.cos(li * dt), mag * jnp.sin(li * dt)
    den = lr * lr + li * li
    q_re = ((bar_re - 1.0) * lr + bar_im * li) / den
    q_im = (bar_im * lr - (bar_re - 1.0) * li) / den
    bb_re = q_re[..., None] * b_re - q_im[..., None] * b_im
    bb_im = q_re[..., None] * b_im + q_im[..., None] * b_re
    eye = jnp.eye(SSM_CHUNK_GROUPS, dtype=F32)

    def in_blocks(w):
        w = w.reshape(SSM_CHUNKS, SSM_CHUNK_GROUPS, SSM_STATE, SSM_GROUP_DIM)
        m = jnp.einsum('cgph,gk->cghkp', w, eye)
        return m.reshape(SSM_CHUNKS, SSM_CHUNK_CH, SSM_CHUNK_ST)

    def out_blocks(w):
        w = w.reshape(SSM_CHUNKS, SSM_CHUNK_GROUPS, SSM_GROUP_DIM, SSM_STATE)
        m = jnp.einsum('cghp,gk->cgpkh', w, eye)
        return m.reshape(SSM_CHUNKS, SSM_CHUNK_ST, SSM_CHUNK_CH)

    bmat = jnp.concatenate([in_blocks(bb_re), in_blocks(bb_im)], axis=2).astype(BF16)
    cmat = jnp.concatenate([out_blocks(c_re), out_blocks(-c_im)], axis=1).astype(BF16)
    lre = bar_re.reshape(SSM_CHUNKS, 1, SSM_CHUNK_ST)
    lim = bar_im.reshape(SSM_CHUNKS, 1, SSM_CHUNK_ST)
    return bmat, cmat, lre, lim


def _layer(h, mixer_stage, mixer_init, mixer_consts, mixer_scratch, ffn_consts, final, name,
           batch_major_in=False, batch_major_out=False):
    n_mix, n_ffn = len(mixer_consts), len(ffn_consts)

    def body(*refs):
        h_ref = refs[0]
        mix_c = refs[1:1 + n_mix]
        ffn_c = refs[1 + n_mix:1 + n_mix + n_ffn]
        o_ref = refs[1 + n_mix + n_ffn]
        scratch = list(refs[2 + n_mix + n_ffn:])
        mid_ref, hn_ref, act_ref = scratch[:3]
        del scratch[:3]
        hin_ref, hin_tiles_ref = (scratch.pop(0), scratch.pop(0)) if batch_major_in else (None, None)
        out_ref, out_tiles_ref = (scratch.pop(0), scratch.pop(0)) if batch_major_out else (None, None)
        mix_s = scratch

        @pl.when(pl.program_id(0) == 0)
        def _():
            mid_ref[...] = jnp.zeros((TILE_M, D_MODEL), F32)
            mixer_init(*mix_s)

        if batch_major_in:
            for c in range(D_MODEL // LANES):
                lanes = slice(c * LANES, (c + 1) * LANES)
                for b in range(BATCH):
                    hin_tiles_ref[c, pl.ds(b, TILE_T, stride=BATCH), :] = h_ref[b, :, lanes]
                hin_ref[:, lanes] = hin_tiles_ref[c]
        h2d_ref = hin_ref if batch_major_in else h_ref
        o2d_ref = out_ref if batch_major_out else o_ref
        ffn_pre, ffn_items, ffn_post = _ffn_stage(mid_ref, *ffn_c, o2d_ref, hn_ref, act_ref, final=final)
        mix_pre, mix_items, mix_post = mixer_stage(h2d_ref, *mix_c, mid_ref, *mix_s)
        ffn_pre()
        mix_pre()
        for thunk in _interleave(ffn_items, mix_items):
            thunk()
        ffn_post()
        mix_post()
        if batch_major_out:
            for c in range(D_MODEL // LANES):
                lanes = slice(c * LANES, (c + 1) * LANES)
                out_tiles_ref[c] = out_ref[:, lanes]
                for b in range(BATCH):
                    o_ref[b, :, lanes] = out_tiles_ref[c, pl.ds(b, TILE_T, stride=BATCH), :]

    def stream_spec(batch_major, index):
        if batch_major:
            return pl.BlockSpec((BATCH, TILE_T, D_MODEL), lambda i: (0, index(i), 0))
        return pl.BlockSpec((TILE_M, D_MODEL), lambda i: (index(i), 0))

    relayout_scratch = [pltpu.VMEM((TILE_M, D_MODEL), F32), pltpu.VMEM((D_MODEL // LANES, TILE_M, LANES), F32)]
    relayout_scratch = relayout_scratch * (int(batch_major_in) + int(batch_major_out))
    consts = tuple(mixer_consts) + tuple(ffn_consts)
    call = pl.pallas_call(
        body,
        grid=(N_TILES + 1,),
        in_specs=[stream_spec(batch_major_in, lambda i: jnp.minimum(i, N_TILES - 1))]
        + [_const_spec(c) for c in consts],
        out_specs=stream_spec(batch_major_out, lambda i: jnp.maximum(i - 1, 0)),
        out_shape=jax.ShapeDtypeStruct((BATCH, SEQ, D_MODEL) if batch_major_out else (ROWS, D_MODEL), F32),
        scratch_shapes=[pltpu.VMEM((TILE_M, D_MODEL), F32),
                        pltpu.VMEM((TILE_M, D_MODEL), BF16),
                        pltpu.VMEM((TILE_M, D_FF), BF16)]
        + relayout_scratch + list(mixer_scratch),
        compiler_params=pltpu.CompilerParams(
            dimension_semantics=("arbitrary",), vmem_limit_bytes=VMEM_LIMIT),
        name=name,
    )
    return call(h, *consts)


def kernel(x, mix_norm, ffn_norm, w_gate, w_up, w_down, pool_w, pool_b, pool_scale, conv_w_in, conv_b_in,
           conv_dw, conv_dw_b, conv_ln_g, conv_ln_b, conv_w_out, ssm_lam_re, ssm_lam_im, ssm_log_dt,
           ssm_b_re, ssm_b_im, ssm_c_re, ssm_c_im, ssm_d, ssm_w_glu_a, ssm_w_glu_b, final_norm):
    assert x.shape == (BATCH, SEQ, D_MODEL) and x.dtype == F32
    row = lambda v: v.reshape(1, -1)
    h = x
    for i in range(DEPTH):
        kind, j = i % N_MIXERS, i // N_MIXERS
        final = i == DEPTH - 1
        g = row(mix_norm[i])
        ffn_consts = (row(ffn_norm[i]), w_gate[i].astype(BF16), w_up[i].astype(BF16),
                      w_down[i].astype(BF16), row(final_norm))
        if kind == 0:
            consts = (g, pool_w[j].astype(BF16), row(pool_b[j]), row(pool_scale[j]))
            scratch = [pltpu.VMEM((POOL_HALO + TILE_M, D_MODEL), F32)]
            h = _layer(h, _pool_stage, _pool_init, consts, scratch, ffn_consts, final, f"layer{i}_pool",
                       batch_major_in=(i == 0), batch_major_out=final)
        elif kind == 1:
            dw8 = jnp.broadcast_to(conv_dw[j][:, None, :], (CONV_WIDTH, 8, D_MODEL))
            dwb8 = jnp.broadcast_to(row(conv_dw_b[j]), (8, D_MODEL))
            consts = (jnp.zeros((1,), jnp.int32), g, conv_w_in[j].astype(BF16), row(conv_b_in[j]), dw8, dwb8,
                      row(conv_ln_g[j]), row(conv_ln_b[j]), conv_w_out[j].astype(BF16))
            scratch = [pltpu.VMEM((TILE_M, D_MODEL), BF16),
                       pltpu.VMEM((CONV_HALO + TILE_M, D_MODEL), F32),
                       pltpu.VMEM((TILE_M, D_MODEL), F32),
                       pltpu.VMEM((TILE_M, D_MODEL), BF16)]
            h = _layer(h, _conv_stage, _conv_init, consts, scratch, ffn_consts, final, f"layer{i}_conv")
        else:
            bmat, cmat, lre, lim = _ssm_params(ssm_lam_re[j], ssm_lam_im[j], ssm_log_dt[j], ssm_b_re[j],
                                               ssm_b_im[j], ssm_c_re[j], ssm_c_im[j])
            consts = (g, bmat, cmat, lre, lim, row(ssm_d[j]),
                      ssm_w_glu_a[j].astype(BF16), ssm_w_glu_b[j].astype(BF16))
            scratch = [pltpu.VMEM((TILE_M, D_MODEL), F32),
                       pltpu.VMEM((2, TILE_M, 2 * SSM_CHUNK_ST), F32),
                       pltpu.VMEM((2, TILE_M, 2 * SSM_CHUNK_ST), BF16),
                       pltpu.VMEM((SSM_CHUNKS, BATCH, 2 * SSM_CHUNK_ST), F32),
                       pltpu.VMEM((TILE_M, D_MODEL), BF16)]
            h = _layer(h, _ssm_stage, _ssm_init, consts, scratch, ffn_consts, final, f"layer{i}_s5")
    return h
```

```python
import jax
import jax.numpy as jnp
from jax import lax
from jax.experimental import pallas as pl
from jax.experimental.pallas import tpu as pltpu

D_MODEL = 1024
BATCH = 16
SEQ = 2048
DEPTH = 4
N_MIXERS = 3
POOL_WINDOWS = (2, 4, 8, 16)
POOL_GROUP_DIM = D_MODEL // len(POOL_WINDOWS)
CONV_WIDTH = 31
SSM_GROUP_DIM = 16
SSM_GROUPS = D_MODEL // SSM_GROUP_DIM
SSM_STATE = 64
D_FF = 2816
RMS_EPS = 1e-6
LN_EPS = 1e-5
LAM_RE_MAX = -1e-4

LANES = 128
ROWS = SEQ * BATCH
TILE_T = 32
TILE_M = TILE_T * BATCH
N_TILES = ROWS // TILE_M
FF_CHUNK = 256
FF_OUT_CHUNK = 256
POOL_HALO = 16 * BATCH
CONV_HALO = 32 * BATCH
CONV_LANES = 128
CONV_TAPS = 8
CONV_CHAINS = 8
CONV_IN_CHUNK = 256
CONV_LN_ROWS = 128
SSM_CHUNK_GROUPS = 8
SSM_CHUNKS = SSM_GROUPS // SSM_CHUNK_GROUPS
SSM_CHUNK_CH = SSM_CHUNK_GROUPS * SSM_GROUP_DIM
SSM_CHUNK_ST = SSM_CHUNK_GROUPS * SSM_STATE
VMEM_LIMIT = 60 * 1024 * 1024

F32 = jnp.float32
BF16 = jnp.bfloat16


def _rmsnorm(x, g):
    return x * lax.rsqrt(jnp.mean(x * x, axis=-1, keepdims=True) + RMS_EPS) * g


def _dot(a, b):
    return jnp.dot(a, b, preferred_element_type=F32)


def _const_spec(c):
    if c.ndim == 1:
        return pl.BlockSpec(memory_space=pltpu.SMEM)
    nd = c.ndim
    return pl.BlockSpec(c.shape, lambda i: (0,) * nd, pipeline_mode=pl.Buffered(1))


def _interleave(a, b):
    out, ia, ib = [], 0, 0
    while ia < len(a) or ib < len(b):
        if ib >= len(b) or (ia < len(a) and ia * len(b) <= ib * len(a)):
            out.append(a[ia])
            ia += 1
        else:
            out.append(b[ib])
            ib += 1
    return out


def _ffn_handoff(mid_ref, g_ref, hn_ref):
    hn_ref[...] = _rmsnorm(mid_ref[...], g_ref[...]).astype(BF16)


def _ffn_stage(mid_ref, g_ref, wg_ref, wu_ref, wd_ref, fin_ref, o_ref, hn_ref, act_ref, *, final):
    def pre():
        o_ref[...] = mid_ref[...]

    def up(c):
        sl = slice(c * FF_CHUNK, (c + 1) * FF_CHUNK)
        gate = _dot(hn_ref[...], wg_ref[:, sl])
        val = _dot(hn_ref[...], wu_ref[:, sl])
        act_ref[:, sl] = (gate * jax.nn.sigmoid(gate) * val).astype(BF16)

    def down(n):
        sl = slice(n * FF_OUT_CHUNK, (n + 1) * FF_OUT_CHUNK)
        o_ref[:, sl] = o_ref[:, sl] + _dot(act_ref[...], wd_ref[:, sl])

    def post():
        if final:
            o_ref[...] = _rmsnorm(o_ref[...], fin_ref[...])

    items = [lambda c=c: up(c) for c in range(D_FF // FF_CHUNK)]
    items += [lambda n=n: down(n) for n in range(D_MODEL // FF_OUT_CHUNK)]
    return pre, items, post


def _pool_init(buf_ref):
    buf_ref[0:POOL_HALO, :] = jnp.zeros((POOL_HALO, D_MODEL), F32)


def _pool_stage(h_ref, g_ref, w_ref, b_ref, s_ref, mid_ref, buf_ref):
    def pre():
        buf_ref[POOL_HALO:, :] = _rmsnorm(h_ref[...], g_ref[...])

    def group(g):
        win = POOL_WINDOWS[g]
        cols = slice(g * POOL_GROUP_DIM, (g + 1) * POOL_GROUP_DIM)
        row = lax.broadcasted_iota(jnp.int32, (TILE_M, 1), 0)
        t = pl.program_id(0) * TILE_T + row // BATCH
        s = buf_ref[:, cols]
        span = 1
        while span < win:
            sh = span * BATCH
            s = s[sh:] + s[:-sh]
            span *= 2
        s = s[s.shape[0] - TILE_M:]
        cnt = jnp.minimum(t + 1, win).astype(F32)
        diff = s / cnt - buf_ref[POOL_HALO:, cols]
        y = (_dot(diff.astype(BF16), w_ref[g]) + b_ref[:, cols]) * s_ref[:, cols]
        mid_ref[:, cols] = h_ref[:, cols] + y

    def post():
        buf_ref[0:POOL_HALO, :] = buf_ref[TILE_M:TILE_M + POOL_HALO, :]

    return pre, [lambda g=g: group(g) for g in range(len(POOL_WINDOWS))], post


def _conv_init(hn_ref, u_ref, v_ref, un_ref):
    u_ref[0:CONV_HALO, :] = jnp.zeros((CONV_HALO, D_MODEL), F32)


def _conv_stage(h_ref, never_ref, g_ref, win_ref, bin_ref, dw_ref, dwb_ref, lng_ref, lnb_ref, wout_ref, mid_ref,
                hn_ref, u_ref, v_ref, un_ref):
    never = never_ref[0] != 0
    finished = []

    def start(init):
        if len(finished) < CONV_CHAINS:
            return init
        return jnp.where(never, finished[-CONV_CHAINS], init)

    def pre():
        hn_ref[...] = _rmsnorm(h_ref[...], g_ref[...]).astype(BF16)

    def glu(n):
        a_cols = slice(n * CONV_IN_CHUNK, (n + 1) * CONV_IN_CHUNK)
        g_cols = slice(D_MODEL + n * CONV_IN_CHUNK, D_MODEL + (n + 1) * CONV_IN_CHUNK)
        a = _dot(hn_ref[...], win_ref[:, a_cols]) + bin_ref[:, a_cols]
        gate = _dot(hn_ref[...], win_ref[:, g_cols]) + bin_ref[:, g_cols]
        u_ref[CONV_HALO:, a_cols] = a * jax.nn.sigmoid(gate)

    def conv(cb):
        finished.clear()
        cols = slice(cb * CONV_LANES, (cb + 1) * CONV_LANES)
        for k0 in range(0, CONV_WIDTH, CONV_TAPS):
            taps = range(k0, min(k0 + CONV_TAPS, CONV_WIDTH))
            ws = {k: dw_ref[k, :, cols] for k in taps}
            loaded = {}

            def src(r):
                if r not in loaded:
                    loaded[r] = u_ref[CONV_HALO + r:CONV_HALO + r + 8, cols]
                return loaded[r]

            for r in range(0, TILE_M, 8):
                acc = start(dwb_ref[:, cols] if k0 == 0 else v_ref[r:r + 8, cols])
                for k in taps:
                    acc = acc + ws[k] * src(r - (CONV_WIDTH - 1 - k) * BATCH)
                v_ref[r:r + 8, cols] = acc
                finished.append(acc)

    def norm(rb):
        rows = slice(rb * CONV_LN_ROWS, (rb + 1) * CONV_LN_ROWS)
        v = v_ref[rows, :]
        mu = jnp.mean(v, axis=-1, keepdims=True)
        vc = v - mu
        var = jnp.mean(vc * vc, axis=-1, keepdims=True)
        un = vc * lax.rsqrt(var + LN_EPS) * lng_ref[...] + lnb_ref[...]
        un_ref[rows, :] = (un * jax.nn.sigmoid(un)).astype(BF16)

    def post():
        mid_ref[...] = h_ref[...] + _dot(un_ref[...], wout_ref[...])
        u_ref[0:CONV_HALO, :] = u_ref[TILE_M:TILE_M + CONV_HALO, :]

    items = [lambda n=n: glu(n) for n in range(D_MODEL // CONV_IN_CHUNK)]
    items += [lambda cb=cb: conv(cb) for cb in range(D_MODEL // CONV_LANES)]
    items += [lambda rb=rb: norm(rb) for rb in range(TILE_M // CONV_LN_ROWS)]
    return pre, items, post


def _ssm_init(u_ref, x_ref, xs_ref, st_ref, gl_ref):
    st_ref[...] = jnp.zeros(st_ref.shape, F32)


def _ssm_stage(h_ref, g_ref, bmat_ref, cmat_ref, lre_ref, lim_ref, d_ref, wa_ref, wb_ref, mid_ref,
               u_ref, x_ref, xs_ref, st_ref, gl_ref):
    re, im = slice(0, SSM_CHUNK_ST), slice(SSM_CHUNK_ST, 2 * SSM_CHUNK_ST)

    def pre():
        u_ref[...] = _rmsnorm(h_ref[...], g_ref[...])

    def drive(c):
        ch = slice(c * SSM_CHUNK_CH, (c + 1) * SSM_CHUNK_CH)
        x_ref[c % 2] = _dot(u_ref[:, ch].astype(BF16), bmat_ref[c])

    def scan(c):
        lre = jnp.broadcast_to(lre_ref[c], (BATCH, SSM_CHUNK_ST))
        lim = jnp.broadcast_to(lim_ref[c], (BATCH, SSM_CHUNK_ST))
        sre, sim = st_ref[c, :, re], st_ref[c, :, im]
        for t in range(TILE_T):
            rows = slice(t * BATCH, (t + 1) * BATCH)
            nre = lre * sre - lim * sim + x_ref[c % 2, rows, re]
            nim = lre * sim + lim * sre + x_ref[c % 2, rows, im]
            xs_ref[c % 2, rows, re] = nre.astype(BF16)
            xs_ref[c % 2, rows, im] = nim.astype(BF16)
            sre, sim = nre, nim
        st_ref[c, :, re] = sre
        st_ref[c, :, im] = sim

    def readout(c):
        ch = slice(c * SSM_CHUNK_CH, (c + 1) * SSM_CHUNK_CH)
        y = _dot(xs_ref[c % 2], cmat_ref[c]) + d_ref[:, ch] * u_ref[:, ch]
        gl_ref[:, ch] = jax.nn.gelu(y, approximate=True).astype(BF16)

    def post():
        gl = gl_ref[...]
        mid_ref[...] = h_ref[...] + _dot(gl, wa_ref[...]) * jax.nn.sigmoid(_dot(gl, wb_ref[...]))

    items = [lambda: drive(0)]
    for c in range(SSM_CHUNKS):
        if c + 1 < SSM_CHUNKS:
            items.append(lambda c=c: drive(c + 1))
        items.append(lambda c=c: scan(c))
        items.append(lambda c=c: readout(c))
    return pre, items, post


def _ssm_params(lam_re, lam_im, log_dt, b_re, b_im, c_re, c_im):
    lr = jnp.minimum(lam_re, LAM_RE_MAX)
    li = lam_im
    dt = jnp.exp(log_dt)[:, None]
    mag = jnp.exp(lr * dt)
    bar_re, bar_im = mag * jnp.cos(li * dt), mag * jnp.sin(li * dt)
    den = lr * lr + li * li
    q_re = ((bar_re - 1.0) * lr + bar_im * li) / den
    q_im = (bar_im * lr - (bar_re - 1.0) * li) / den
    bb_re = q_re[..., None] * b_re - q_im[..., None] * b_im
    bb_im = q_re[..., None] * b_im + q_im[..., None] * b_re
    eye = jnp.eye(SSM_CHUNK_GROUPS, dtype=F32)

    def in_blocks(w):
        w = w.reshape(SSM_CHUNKS, SSM_CHUNK_GROUPS, SSM_STATE, SSM_GROUP_DIM)
        m = jnp.einsum('cgph,gk->cghkp', w, eye)
        return m.reshape(SSM_CHUNKS, SSM_CHUNK_CH, SSM_CHUNK_ST)

    def out_blocks(w):
        w = w.reshape(SSM_CHUNKS, SSM_CHUNK_GROUPS, SSM_GROUP_DIM, SSM_STATE)
        m = jnp.einsum('cghp,gk->cgpkh', w, eye)
        return m.reshape(SSM_CHUNKS, SSM_CHUNK_ST, SSM_CHUNK_CH)

    bmat = jnp.concatenate([in_blocks(bb_re), in_blocks(bb_im)], axis=2).astype(BF16)
    cmat = jnp.concatenate([out_blocks(c_re), out_blocks(-c_im)], axis=1).astype(BF16)
    lre = bar_re.reshape(SSM_CHUNKS, 1, SSM_CHUNK_ST)
    lim = bar_im.reshape(SSM_CHUNKS, 1, SSM_CHUNK_ST)
    return bmat, cmat, lre, lim


def _layer(h, mixer_stage, mixer_init, mixer_consts, mixer_scratch, ffn_consts, final, name,
           batch_major_in=False, batch_major_out=False):
    n_mix, n_ffn = len(mixer_consts), len(ffn_consts)

    def body(*refs):
        h_ref = refs[0]
        mix_c = refs[1:1 + n_mix]
        ffn_c = refs[1 + n_mix:1 + n_mix + n_ffn]
        o_ref = refs[1 + n_mix + n_ffn]
        scratch = list(refs[2 + n_mix + n_ffn:])
        mid_ref, hn_ref, act_ref = scratch[:3]
        del scratch[:3]
        hin_ref, hin_tiles_ref = (scratch.pop(0), scratch.pop(0)) if batch_major_in else (None, None)
        out_ref, out_tiles_ref = (scratch.pop(0), scratch.pop(0)) if batch_major_out else (None, None)
        mix_s = scratch

        h2d_ref = hin_ref if batch_major_in else h_ref
        o2d_ref = out_ref if batch_major_out else o_ref

        def step(run_mixer, run_ffn):
            if run_mixer and batch_major_in:
                for c in range(D_MODEL // LANES):
                    lanes = slice(c * LANES, (c + 1) * LANES)
                    for b in range(BATCH):
                        hin_tiles_ref[c, pl.ds(b, TILE_T, stride=BATCH), :] = h_ref[b, :, lanes]
                    hin_ref[:, lanes] = hin_tiles_ref[c]
            none = (lambda: None, [], lambda: None)
            ffn_pre, ffn_items, ffn_post = none
            mix_pre, mix_items, mix_post = none
            if run_ffn:
                ffn_pre, ffn_items, ffn_post = _ffn_stage(mid_ref, *ffn_c, o2d_ref, hn_ref, act_ref, final=final)
            if run_mixer:
                mix_pre, mix_items, mix_post = mixer_stage(h2d_ref, *mix_c, mid_ref, *mix_s)
            ffn_pre()
            mix_pre()
            for thunk in _interleave(ffn_items, mix_items):
                thunk()
            ffn_post()
            mix_post()
            if run_mixer:
                _ffn_handoff(mid_ref, ffn_c[0], hn_ref)
            if run_ffn and batch_major_out:
                for c in range(D_MODEL // LANES):
                    lanes = slice(c * LANES, (c + 1) * LANES)
                    out_tiles_ref[c] = out_ref[:, lanes]
                    for b in range(BATCH):
                        o_ref[b, :, lanes] = out_tiles_ref[c, pl.ds(b, TILE_T, stride=BATCH), :]

        i = pl.program_id(0)

        @pl.when(i == 0)
        def _():
            mixer_init(*mix_s)
            o_ref[...] = jnp.zeros(o_ref.shape, F32)
            step(True, False)

        @pl.when(jnp.logical_and(i > 0, i < N_TILES))
        def _():
            step(True, True)

        @pl.when(i == N_TILES)
        def _():
            step(False, True)

    def stream_spec(batch_major, index):
        if batch_major:
            return pl.BlockSpec((BATCH, TILE_T, D_MODEL), lambda i: (0, index(i), 0))
        return pl.BlockSpec((TILE_M, D_MODEL), lambda i: (index(i), 0))

    relayout_scratch = [pltpu.VMEM((TILE_M, D_MODEL), F32), pltpu.VMEM((D_MODEL // LANES, TILE_M, LANES), F32)]
    relayout_scratch = relayout_scratch * (int(batch_major_in) + int(batch_major_out))
    consts = tuple(mixer_consts) + tuple(ffn_consts)
    call = pl.pallas_call(
        body,
        grid=(N_TILES + 1,),
        in_specs=[stream_spec(batch_major_in, lambda i: jnp.minimum(i, N_TILES - 1))]
        + [_const_spec(c) for c in consts],
        out_specs=stream_spec(batch_major_out, lambda i: jnp.maximum(i - 1, 0)),
        out_shape=jax.ShapeDtypeStruct((BATCH, SEQ, D_MODEL) if batch_major_out else (ROWS, D_MODEL), F32),
        scratch_shapes=[pltpu.VMEM((TILE_M, D_MODEL), F32),
                        pltpu.VMEM((TILE_M, D_MODEL), BF16),
                        pltpu.VMEM((TILE_M, D_FF), BF16)]
        + relayout_scratch + list(mixer_scratch),
        compiler_params=pltpu.CompilerParams(
            dimension_semantics=("arbitrary",), vmem_limit_bytes=VMEM_LIMIT),
        name=name,
    )
    return call(h, *consts)


def kernel(x, mix_norm, ffn_norm, w_gate, w_up, w_down, pool_w, pool_b, pool_scale, conv_w_in, conv_b_in,
           conv_dw, conv_dw_b, conv_ln_g, conv_ln_b, conv_w_out, ssm_lam_re, ssm_lam_im, ssm_log_dt,
           ssm_b_re, ssm_b_im, ssm_c_re, ssm_c_im, ssm_d, ssm_w_glu_a, ssm_w_glu_b, final_norm):
    assert x.shape == (BATCH, SEQ, D_MODEL) and x.dtype == F32
    row = lambda v: v.reshape(1, -1)
    h = x
    for i in range(DEPTH):
        kind, j = i % N_MIXERS, i // N_MIXERS
        final = i == DEPTH - 1
        g = row(mix_norm[i])
        ffn_consts = (row(ffn_norm[i]), w_gate[i].astype(BF16), w_up[i].astype(BF16),
                      w_down[i].astype(BF16), row(final_norm))
        if kind == 0:
            consts = (g, pool_w[j].astype(BF16), row(pool_b[j]), row(pool_scale[j]))
            scratch = [pltpu.VMEM((POOL_HALO + TILE_M, D_MODEL), F32)]
            h = _layer(h, _pool_stage, _pool_init, consts, scratch, ffn_consts, final, f"layer{i}_pool",
                       batch_major_in=(i == 0), batch_major_out=final)
        elif kind == 1:
            dw8 = jnp.broadcast_to(conv_dw[j][:, None, :], (CONV_WIDTH, 8, D_MODEL))
            dwb8 = jnp.broadcast_to(row(conv_dw_b[j]), (8, D_MODEL))
            consts = (jnp.zeros((1,), jnp.int32), g, conv_w_in[j].astype(BF16), row(conv_b_in[j]), dw8, dwb8,
                      row(conv_ln_g[j]), row(conv_ln_b[j]), conv_w_out[j].astype(BF16))
            scratch = [pltpu.VMEM((TILE_M, D_MODEL), BF16),
                       pltpu.VMEM((CONV_HALO + TILE_M, D_MODEL), F32),
                       pltpu.VMEM((TILE_M, D_MODEL), F32),
                       pltpu.VMEM((TILE_M, D_MODEL), BF16)]
            h = _layer(h, _conv_stage, _conv_init, consts, scratch, ffn_consts, final, f"layer{i}_conv")
        else:
            bmat, cmat, lre, lim = _ssm_params(ssm_lam_re[j], ssm_lam_im[j], ssm_log_dt[j], ssm_b_re[j],
                                               ssm_b_im[j], ssm_c_re[j], ssm_c_im[j])
            consts = (g, bmat, cmat, lre, lim, row(ssm_d[j]),
                      ssm_w_glu_a[j].astype(BF16), ssm_w_glu_b[j].astype(BF16))
            scratch = [pltpu.VMEM((TILE_M, D_MODEL), F32),
                       pltpu.VMEM((2, TILE_M, 2 * SSM_CHUNK_ST), F32),
                       pltpu.VMEM((2, TILE_M, 2 * SSM_CHUNK_ST), BF16),
                       pltpu.VMEM((SSM_CHUNKS, BATCH, 2 * SSM_CHUNK_ST), F32),
                       pltpu.VMEM((TILE_M, D_MODEL), BF16)]
            h = _layer(h, _ssm_stage, _ssm_init, consts, scratch, ffn_consts, final, f"layer{i}_s5")
    return h
```

```python
import jax
import jax.numpy as jnp
from jax import lax
from jax.experimental import pallas as pl
from jax.experimental.pallas import tpu as pltpu

D_MODEL = 1024
BATCH = 16
SEQ = 2048
DEPTH = 4
N_MIXERS = 3
POOL_WINDOWS = (2, 4, 8, 16)
POOL_GROUP_DIM = D_MODEL // len(POOL_WINDOWS)
CONV_WIDTH = 31
SSM_GROUP_DIM = 16
SSM_GROUPS = D_MODEL // SSM_GROUP_DIM
SSM_STATE = 64
D_FF = 2816
RMS_EPS = 1e-6
LN_EPS = 1e-5
LAM_RE_MAX = -1e-4

LANES = 128
ROWS = SEQ * BATCH
TILE_T = 32
TILE_M = TILE_T * BATCH
N_TILES = ROWS // TILE_M
FF_CHUNK = 256
FF_OUT_CHUNK = 256
POOL_HALO = 16 * BATCH
CONV_HALO = 32 * BATCH
CONV_LANES = 128
CONV_TAPS = 8
CONV_CHAINS = 8
CONV_IN_CHUNK = 256
CONV_LN_ROWS = 128
SSM_CHUNK_GROUPS = 8
SSM_CHUNKS = SSM_GROUPS // SSM_CHUNK_GROUPS
SSM_CHUNK_CH = SSM_CHUNK_GROUPS * SSM_GROUP_DIM
SSM_CHUNK_ST = SSM_CHUNK_GROUPS * SSM_STATE
VMEM_LIMIT = 60 * 1024 * 1024

F32 = jnp.float32
BF16 = jnp.bfloat16


def _rmsnorm(x, g):
    return x * lax.rsqrt(jnp.mean(x * x, axis=-1, keepdims=True) + RMS_EPS) * g


def _dot(a, b):
    return jnp.dot(a, b, preferred_element_type=F32)


def _const_spec(c, layer=None):
    if c.ndim == 1:
        return pl.BlockSpec(memory_space=pltpu.SMEM)
    if layer is not None:
        nd = c.ndim - 1
        return pl.BlockSpec((None,) + c.shape[1:], lambda i: (layer,) + (0,) * nd, pipeline_mode=pl.Buffered(1))
    nd = c.ndim
    return pl.BlockSpec(c.shape, lambda i: (0,) * nd, pipeline_mode=pl.Buffered(1))


def _interleave(a, b):
    out, ia, ib = [], 0, 0
    while ia < len(a) or ib < len(b):
        if ib >= len(b) or (ia < len(a) and ia * len(b) <= ib * len(a)):
            out.append(a[ia])
            ia += 1
        else:
            out.append(b[ib])
            ib += 1
    return out


def _ffn_stage(mid_ref, g_ref, wg_ref, wu_ref, wd_ref, fin_ref, o_ref, hn_ref, act_ref, *, final):
    def pre():
        m = mid_ref[...]
        o_ref[...] = m
        hn_ref[...] = _rmsnorm(m, g_ref[...]).astype(BF16)

    def up(c):
        sl = slice(c * FF_CHUNK, (c + 1) * FF_CHUNK)
        gate = _dot(hn_ref[...], wg_ref[:, sl])
        val = _dot(hn_ref[...], wu_ref[:, sl])
        act_ref[:, sl] = (gate * jax.nn.sigmoid(gate) * val).astype(BF16)

    def down(n):
        sl = slice(n * FF_OUT_CHUNK, (n + 1) * FF_OUT_CHUNK)
        o_ref[:, sl] = o_ref[:, sl] + _dot(act_ref[...], wd_ref[:, sl])

    def post():
        if final:
            o_ref[...] = _rmsnorm(o_ref[...], fin_ref[...])

    items = [lambda c=c: up(c) for c in range(D_FF // FF_CHUNK)]
    items += [lambda n=n: down(n) for n in range(D_MODEL // FF_OUT_CHUNK)]
    return pre, items, post


def _pool_init(buf_ref):
    buf_ref[0:POOL_HALO, :] = jnp.zeros((POOL_HALO, D_MODEL), F32)


def _pool_stage(h_ref, g_ref, w_ref, b_ref, s_ref, mid_ref, buf_ref):
    def pre():
        buf_ref[POOL_HALO:, :] = _rmsnorm(h_ref[...], g_ref[...])

    def group(g):
        win = POOL_WINDOWS[g]
        cols = slice(g * POOL_GROUP_DIM, (g + 1) * POOL_GROUP_DIM)
        row = lax.broadcasted_iota(jnp.int32, (TILE_M, 1), 0)
        t = pl.program_id(0) * TILE_T + row // BATCH
        s = buf_ref[:, cols]
        span = 1
        while span < win:
            sh = span * BATCH
            s = s[sh:] + s[:-sh]
            span *= 2
        s = s[s.shape[0] - TILE_M:]
        cnt = jnp.minimum(t + 1, win).astype(F32)
        diff = s / cnt - buf_ref[POOL_HALO:, cols]
        y = (_dot(diff.astype(BF16), w_ref[g]) + b_ref[:, cols]) * s_ref[:, cols]
        mid_ref[:, cols] = h_ref[:, cols] + y

    def post():
        buf_ref[0:POOL_HALO, :] = buf_ref[TILE_M:TILE_M + POOL_HALO, :]

    return pre, [lambda g=g: group(g) for g in range(len(POOL_WINDOWS))], post


def _conv_init(hn_ref, u_ref, v_ref, un_ref):
    u_ref[0:CONV_HALO, :] = jnp.zeros((CONV_HALO, D_MODEL), F32)


def _conv_stage(h_ref, never_ref, g_ref, win_ref, bin_ref, dw_ref, dwb_ref, lng_ref, lnb_ref, wout_ref, mid_ref,
                hn_ref, u_ref, v_ref, un_ref):
    never = never_ref[0] != 0
    finished = []

    def start(init):
        if len(finished) < CONV_CHAINS:
            return init
        return jnp.where(never, finished[-CONV_CHAINS], init)

    def pre():
        hn_ref[...] = _rmsnorm(h_ref[...], g_ref[...]).astype(BF16)

    def glu(n):
        a_cols = slice(n * CONV_IN_CHUNK, (n + 1) * CONV_IN_CHUNK)
        g_cols = slice(D_MODEL + n * CONV_IN_CHUNK, D_MODEL + (n + 1) * CONV_IN_CHUNK)
        a = _dot(hn_ref[...], win_ref[:, a_cols]) + bin_ref[:, a_cols]
        gate = _dot(hn_ref[...], win_ref[:, g_cols]) + bin_ref[:, g_cols]
        u_ref[CONV_HALO:, a_cols] = a * jax.nn.sigmoid(gate)

    def conv(cb):
        cols = slice(cb * CONV_LANES, (cb + 1) * CONV_LANES)
        for k0 in range(0, CONV_WIDTH, CONV_TAPS):
            taps = range(k0, min(k0 + CONV_TAPS, CONV_WIDTH))
            ws = {k: dw_ref[k, :, cols] for k in taps}
            loaded = {}

            def src(r):
                if r not in loaded:
                    loaded[r] = u_ref[CONV_HALO + r:CONV_HALO + r + 8, cols]
                return loaded[r]

            for r in range(0, TILE_M, 8):
                acc = start(dwb_ref[:, cols] if k0 == 0 else v_ref[r:r + 8, cols])
                for k in taps:
                    acc = acc + ws[k] * src(r - (CONV_WIDTH - 1 - k) * BATCH)
                v_ref[r:r + 8, cols] = acc
                finished.append(acc)

    def norm(rb):
        rows = slice(rb * CONV_LN_ROWS, (rb + 1) * CONV_LN_ROWS)
        v = v_ref[rows, :]
        mu = jnp.mean(v, axis=-1, keepdims=True)
        vc = v - mu
        var = jnp.mean(vc * vc, axis=-1, keepdims=True)
        un = vc * lax.rsqrt(var + LN_EPS) * lng_ref[...] + lnb_ref[...]
        un_ref[rows, :] = (un * jax.nn.sigmoid(un)).astype(BF16)

    def post():
        mid_ref[...] = h_ref[...] + _dot(un_ref[...], wout_ref[...])
        u_ref[0:CONV_HALO, :] = u_ref[TILE_M:TILE_M + CONV_HALO, :]

    items = [lambda n=n: glu(n) for n in range(D_MODEL // CONV_IN_CHUNK)]
    items += [lambda cb=cb: conv(cb) for cb in range(D_MODEL // CONV_LANES)]
    items += [lambda rb=rb: norm(rb) for rb in range(TILE_M // CONV_LN_ROWS)]
    return pre, items, post


def _ssm_init(u_ref, x_ref, xs_ref, st_ref, gl_ref):
    st_ref[...] = jnp.zeros(st_ref.shape, F32)


def _ssm_stage(h_ref, g_ref, bmat_ref, cmat_ref, lre_ref, lim_ref, d_ref, wa_ref, wb_ref, mid_ref,
               u_ref, x_ref, xs_ref, st_ref, gl_ref):
    re, im = slice(0, SSM_CHUNK_ST), slice(SSM_CHUNK_ST, 2 * SSM_CHUNK_ST)

    def pre():
        u_ref[...] = _rmsnorm(h_ref[...], g_ref[...])

    def drive(c):
        ch = slice(c * SSM_CHUNK_CH, (c + 1) * SSM_CHUNK_CH)
        x_ref[c % 2] = _dot(u_ref[:, ch].astype(BF16), bmat_ref[c])

    def scan(c):
        lre = jnp.broadcast_to(lre_ref[c], (BATCH, SSM_CHUNK_ST))
        lim = jnp.broadcast_to(lim_ref[c], (BATCH, SSM_CHUNK_ST))
        sre, sim = st_ref[c, :, re], st_ref[c, :, im]
        for t in range(TILE_T):
            rows = slice(t * BATCH, (t + 1) * BATCH)
            nre = lre * sre - lim * sim + x_ref[c % 2, rows, re]
            nim = lre * sim + lim * sre + x_ref[c % 2, rows, im]
            xs_ref[c % 2, rows, re] = nre.astype(BF16)
            xs_ref[c % 2, rows, im] = nim.astype(BF16)
            sre, sim = nre, nim
        st_ref[c, :, re] = sre
        st_ref[c, :, im] = sim

    def readout(c):
        ch = slice(c * SSM_CHUNK_CH, (c + 1) * SSM_CHUNK_CH)
        y = _dot(xs_ref[c % 2], cmat_ref[c]) + d_ref[:, ch] * u_ref[:, ch]
        gl_ref[:, ch] = jax.nn.gelu(y, approximate=True).astype(BF16)

    def post():
        gl = gl_ref[...]
        mid_ref[...] = h_ref[...] + _dot(gl, wa_ref[...]) * jax.nn.sigmoid(_dot(gl, wb_ref[...]))

    items = [lambda: drive(0)]
    for c in range(SSM_CHUNKS):
        if c + 1 < SSM_CHUNKS:
            items.append(lambda c=c: drive(c + 1))
        items.append(lambda c=c: scan(c))
        items.append(lambda c=c: readout(c))
    return pre, items, post


def _ssm_params(lam_re, lam_im, log_dt, b_re, b_im, c_re, c_im):
    lr = jnp.minimum(lam_re, LAM_RE_MAX)
    li = lam_im
    dt = jnp.exp(log_dt)[:, None]
    mag = jnp.exp(lr * dt)
    bar_re, bar_im = mag * jnp.cos(li * dt), mag * jnp.sin(li * dt)
    den = lr * lr + li * li
    q_re = ((bar_re - 1.0) * lr + bar_im * li) / den
    q_im = (bar_im * lr - (bar_re - 1.0) * li) / den
    bb_re = q_re[..., None] * b_re - q_im[..., None] * b_im
    bb_im = q_re[..., None] * b_im + q_im[..., None] * b_re
    eye = jnp.eye(SSM_CHUNK_GROUPS, dtype=F32)

    def in_blocks(w):
        w = w.reshape(SSM_CHUNKS, SSM_CHUNK_GROUPS, SSM_STATE, SSM_GROUP_DIM)
        m = jnp.einsum('cgph,gk->cghkp', w, eye)
        return m.reshape(SSM_CHUNKS, SSM_CHUNK_CH, SSM_CHUNK_ST)

    def out_blocks(w):
        w = w.reshape(SSM_CHUNKS, SSM_CHUNK_GROUPS, SSM_GROUP_DIM, SSM_STATE)
        m = jnp.einsum('cghp,gk->cgpkh', w, eye)
        return m.reshape(SSM_CHUNKS, SSM_CHUNK_ST, SSM_CHUNK_CH)

    bmat = jnp.concatenate([in_blocks(bb_re), in_blocks(bb_im)], axis=2).astype(BF16)
    cmat = jnp.concatenate([out_blocks(c_re), out_blocks(-c_im)], axis=1).astype(BF16)
    lre = bar_re.reshape(SSM_CHUNKS, 1, SSM_CHUNK_ST)
    lim = bar_im.reshape(SSM_CHUNKS, 1, SSM_CHUNK_ST)
    return bmat, cmat, lre, lim


def _layer(h, layer, mixer_stage, mixer_init, mixer_consts, mixer_scratch, ffn_consts, name,
           batch_major_in=False, batch_major_out=False):
    n_mix, n_ffn = len(mixer_consts), len(ffn_consts)
    final = layer == DEPTH - 1

    def body(*refs):
        h_ref = refs[0]
        mix_c = refs[1:1 + n_mix]
        ffn_c = refs[1 + n_mix:1 + n_mix + n_ffn]
        o_ref = refs[1 + n_mix + n_ffn]
        scratch = list(refs[2 + n_mix + n_ffn:])
        mid_ref, hn_ref, act_ref = scratch[:3]
        del scratch[:3]
        hin_ref, hin_tiles_ref = (scratch.pop(0), scratch.pop(0)) if batch_major_in else (None, None)
        out_ref, out_tiles_ref = (scratch.pop(0), scratch.pop(0)) if batch_major_out else (None, None)
        mix_s = scratch

        @pl.when(pl.program_id(0) == 0)
        def _():
            mid_ref[...] = jnp.zeros((TILE_M, D_MODEL), F32)
            mixer_init(*mix_s)

        if batch_major_in:
            for c in range(D_MODEL // LANES):
                lanes = slice(c * LANES, (c + 1) * LANES)
                for b in range(BATCH):
                    hin_tiles_ref[c, pl.ds(b, TILE_T, stride=BATCH), :] = h_ref[b, :, lanes]
                hin_ref[:, lanes] = hin_tiles_ref[c]
        h2d_ref = hin_ref if batch_major_in else h_ref
        o2d_ref = out_ref if batch_major_out else o_ref
        ffn_pre, ffn_items, ffn_post = _ffn_stage(mid_ref, *ffn_c, o2d_ref, hn_ref, act_ref, final=final)
        mix_pre, mix_items, mix_post = mixer_stage(h2d_ref, *mix_c, mid_ref, *mix_s)
        ffn_pre()
        mix_pre()
        for thunk in _interleave(ffn_items, mix_items):
            thunk()
        ffn_post()
        mix_post()
        if batch_major_out:
            for c in range(D_MODEL // LANES):
                lanes = slice(c * LANES, (c + 1) * LANES)
                out_tiles_ref[c] = out_ref[:, lanes]
                for b in range(BATCH):
                    o_ref[b, :, lanes] = out_tiles_ref[c, pl.ds(b, TILE_T, stride=BATCH), :]

    def stream_spec(batch_major, index):
        if batch_major:
            return pl.BlockSpec((BATCH, TILE_T, D_MODEL), lambda i: (0, index(i), 0))
        return pl.BlockSpec((TILE_M, D_MODEL), lambda i: (index(i), 0))

    relayout_scratch = [pltpu.VMEM((TILE_M, D_MODEL), F32), pltpu.VMEM((D_MODEL // LANES, TILE_M, LANES), F32)]
    relayout_scratch = relayout_scratch * (int(batch_major_in) + int(batch_major_out))
    consts = tuple(mixer_consts) + tuple(ffn_consts)
    stacked = (None,) * n_mix + tuple(layer if c.ndim == 3 else None for c in ffn_consts)
    call = pl.pallas_call(
        body,
        grid=(N_TILES + 1,),
        in_specs=[stream_spec(batch_major_in, lambda i: jnp.minimum(i, N_TILES - 1))]
        + [_const_spec(c, s) for c, s in zip(consts, stacked)],
        out_specs=stream_spec(batch_major_out, lambda i: jnp.maximum(i - 1, 0)),
        out_shape=jax.ShapeDtypeStruct((BATCH, SEQ, D_MODEL) if batch_major_out else (ROWS, D_MODEL), F32),
        scratch_shapes=[pltpu.VMEM((TILE_M, D_MODEL), F32),
                        pltpu.VMEM((TILE_M, D_MODEL), BF16),
                        pltpu.VMEM((TILE_M, D_FF), BF16)]
        + relayout_scratch + list(mixer_scratch),
        compiler_params=pltpu.CompilerParams(
            dimension_semantics=("arbitrary",), vmem_limit_bytes=VMEM_LIMIT),
        name=name,
    )
    return call(h, *consts)


def kernel(x, mix_norm, ffn_norm, w_gate, w_up, w_down, pool_w, pool_b, pool_scale, conv_w_in, conv_b_in,
           conv_dw, conv_dw_b, conv_ln_g, conv_ln_b, conv_w_out, ssm_lam_re, ssm_lam_im, ssm_log_dt,
           ssm_b_re, ssm_b_im, ssm_c_re, ssm_c_im, ssm_d, ssm_w_glu_a, ssm_w_glu_b, final_norm):
    assert x.shape == (BATCH, SEQ, D_MODEL) and x.dtype == F32
    row = lambda v: v.reshape(1, -1)
    h = x
    w_gate, w_up, w_down = w_gate.astype(BF16), w_up.astype(BF16), w_down.astype(BF16)
    for i in range(DEPTH):
        kind, j = i % N_MIXERS, i // N_MIXERS
        g = row(mix_norm[i])
        ffn_consts = (row(ffn_norm[i]), w_gate, w_up, w_down, row(final_norm))
        if kind == 0:
            consts = (g, pool_w[j].astype(BF16), row(pool_b[j]), row(pool_scale[j]))
            scratch = [pltpu.VMEM((POOL_HALO + TILE_M, D_MODEL), F32)]
            h = _layer(h, i, _pool_stage, _pool_init, consts, scratch, ffn_consts, f"layer{i}_pool",
                       batch_major_in=(i == 0), batch_major_out=(i == DEPTH - 1))
        elif kind == 1:
            dw8 = jnp.broadcast_to(conv_dw[j][:, None, :], (CONV_WIDTH, 8, D_MODEL))
            dwb8 = jnp.broadcast_to(row(conv_dw_b[j]), (8, D_MODEL))
            consts = (jnp.zeros((1,), jnp.int32), g, conv_w_in[j].astype(BF16), row(conv_b_in[j]), dw8, dwb8,
                      row(conv_ln_g[j]), row(conv_ln_b[j]), conv_w_out[j].astype(BF16))
            scratch = [pltpu.VMEM((TILE_M, D_MODEL), BF16),
                       pltpu.VMEM((CONV_HALO + TILE_M, D_MODEL), F32),
                       pltpu.VMEM((TILE_M, D_MODEL), F32),
                       pltpu.VMEM((TILE_M, D_MODEL), BF16)]
            h = _layer(h, i, _conv_stage, _conv_init, consts, scratch, ffn_consts, f"layer{i}_conv")
        else:
            bmat, cmat, lre, lim = _ssm_params(ssm_lam_re[j], ssm_lam_im[j], ssm_log_dt[j], ssm_b_re[j],
                                               ssm_b_im[j], ssm_c_re[j], ssm_c_im[j])
            consts = (g, bmat, cmat, lre, lim, row(ssm_d[j]),
                      ssm_w_glu_a[j].astype(BF16), ssm_w_glu_b[j].astype(BF16))
            scratch = [pltpu.VMEM((TILE_M, D_MODEL), F32),
                       pltpu.VMEM((2, TILE_M, 2 * SSM_CHUNK_ST), F32),
                       pltpu.VMEM((2, TILE_M, 2 * SSM_CHUNK_ST), BF16),
                       pltpu.VMEM((SSM_CHUNKS, BATCH, 2 * SSM_CHUNK_ST), F32),
                       pltpu.VMEM((TILE_M, D_MODEL), BF16)]
            h = _layer(h, i, _ssm_stage, _ssm_init, consts, scratch, ffn_consts, f"layer{i}_s5")
    return h
```

```python
import jax
import jax.numpy as jnp
from jax import lax
from jax.experimental import pallas as pl
from jax.experimental.pallas import tpu as pltpu

D_MODEL = 1024
BATCH = 16
SEQ = 2048
DEPTH = 4
N_MIXERS = 3
POOL_WINDOWS = (2, 4, 8, 16)
POOL_GROUP_DIM = D_MODEL // len(POOL_WINDOWS)
CONV_WIDTH = 31
SSM_GROUP_DIM = 16
SSM_GROUPS = D_MODEL // SSM_GROUP_DIM
SSM_STATE = 64
D_FF = 2816
RMS_EPS = 1e-6
LN_EPS = 1e-5
LAM_RE_MAX = -1e-4

LANES = 128
ROWS = SEQ * BATCH
TILE_T = 32
TILE_M = TILE_T * BATCH
N_TILES = ROWS // TILE_M
FF_CHUNK = 256
FF_OUT_CHUNK = 256
POOL_HALO = 16 * BATCH
CONV_HALO = 32 * BATCH
CONV_LANES = 128
CONV_TAPS = 8
CONV_CHAINS = 8
CONV_IN_CHUNK = 256
CONV_LN_ROWS = 128
SSM_CHUNK_GROUPS = 8
SSM_CHUNKS = SSM_GROUPS // SSM_CHUNK_GROUPS
SSM_CHUNK_CH = SSM_CHUNK_GROUPS * SSM_GROUP_DIM
SSM_CHUNK_ST = SSM_CHUNK_GROUPS * SSM_STATE
VMEM_LIMIT = 60 * 1024 * 1024

F32 = jnp.float32
BF16 = jnp.bfloat16


def _rmsnorm(x, g):
    return x * lax.rsqrt(jnp.mean(x * x, axis=-1, keepdims=True) + RMS_EPS) * g


def _dot(a, b):
    return jnp.dot(a, b, preferred_element_type=F32)


def _const_spec(c, layer=None):
    if c.ndim == 1:
        return pl.BlockSpec(memory_space=pltpu.SMEM)
    if layer is not None:
        nd = c.ndim - 1
        return pl.BlockSpec((None,) + c.shape[1:], lambda i: (layer,) + (0,) * nd, pipeline_mode=pl.Buffered(1))
    nd = c.ndim
    return pl.BlockSpec(c.shape, lambda i: (0,) * nd, pipeline_mode=pl.Buffered(1))


_LINK = {}


def _interleave(a, b):
    out, ia, ib = [], 0, 0
    while ia < len(a) or ib < len(b):
        if ib >= len(b) or (ia < len(a) and ia * len(b) <= ib * len(a)):
            out.append(a[ia])
            ia += 1
        else:
            out.append(b[ib])
            ib += 1
    return out


def _ffn_stage(mid_ref, g_ref, wg_ref, wu_ref, wd_ref, fin_ref, o_ref, hn_ref, act_ref, *, final):
    def pre():
        m = mid_ref[...]
        o_ref[...] = m
        hn_ref[...] = _rmsnorm(m, g_ref[...]).astype(BF16)

    def up(c):
        sl = slice(c * FF_CHUNK, (c + 1) * FF_CHUNK)
        gate = _dot(hn_ref[...], wg_ref[:, sl])
        val = _dot(hn_ref[...], wu_ref[:, sl])
        if 'conv' in _LINK:
            gate = jnp.where(_LINK['never'], _LINK['conv'][0:1, 0:1], gate)
        _LINK['ffn'] = gate[0:8, 0:LANES]
        act_ref[:, sl] = (gate * jax.nn.sigmoid(gate) * val).astype(BF16)

    def down(n):
        sl = slice(n * FF_OUT_CHUNK, (n + 1) * FF_OUT_CHUNK)
        o_ref[:, sl] = o_ref[:, sl] + _dot(act_ref[...], wd_ref[:, sl])

    def post():
        if final:
            o_ref[...] = _rmsnorm(o_ref[...], fin_ref[...])

    items = [lambda c=c: up(c) for c in range(D_FF // FF_CHUNK)]
    items += [lambda n=n: down(n) for n in range(D_MODEL // FF_OUT_CHUNK)]
    return pre, items, post


def _pool_init(buf_ref):
    buf_ref[0:POOL_HALO, :] = jnp.zeros((POOL_HALO, D_MODEL), F32)


def _pool_stage(h_ref, g_ref, w_ref, b_ref, s_ref, mid_ref, buf_ref):
    def pre():
        buf_ref[POOL_HALO:, :] = _rmsnorm(h_ref[...], g_ref[...])

    def group(g):
        win = POOL_WINDOWS[g]
        cols = slice(g * POOL_GROUP_DIM, (g + 1) * POOL_GROUP_DIM)
        row = lax.broadcasted_iota(jnp.int32, (TILE_M, 1), 0)
        t = pl.program_id(0) * TILE_T + row // BATCH
        s = buf_ref[:, cols]
        span = 1
        while span < win:
            sh = span * BATCH
            s = s[sh:] + s[:-sh]
            span *= 2
        s = s[s.shape[0] - TILE_M:]
        cnt = jnp.minimum(t + 1, win).astype(F32)
        diff = s / cnt - buf_ref[POOL_HALO:, cols]
        y = (_dot(diff.astype(BF16), w_ref[g]) + b_ref[:, cols]) * s_ref[:, cols]
        mid_ref[:, cols] = h_ref[:, cols] + y

    def post():
        buf_ref[0:POOL_HALO, :] = buf_ref[TILE_M:TILE_M + POOL_HALO, :]

    return pre, [lambda g=g: group(g) for g in range(len(POOL_WINDOWS))], post


def _conv_init(hn_ref, u_ref, v_ref, un_ref):
    u_ref[0:CONV_HALO, :] = jnp.zeros((CONV_HALO, D_MODEL), F32)


def _conv_stage(h_ref, never_ref, g_ref, win_ref, bin_ref, dw_ref, dwb_ref, lng_ref, lnb_ref, wout_ref, mid_ref,
                hn_ref, u_ref, v_ref, un_ref):
    never = never_ref[0] != 0
    finished = []

    _LINK['never'] = never

    def start(init):
        if _LINK.pop('fresh', False) and 'ffn' in _LINK:
            return jnp.where(never, _LINK['ffn'], init)
        if len(finished) < CONV_CHAINS:
            return init
        return jnp.where(never, finished[-CONV_CHAINS], init)

    def pre():
        hn_ref[...] = _rmsnorm(h_ref[...], g_ref[...]).astype(BF16)

    def glu(n):
        a_cols = slice(n * CONV_IN_CHUNK, (n + 1) * CONV_IN_CHUNK)
        g_cols = slice(D_MODEL + n * CONV_IN_CHUNK, D_MODEL + (n + 1) * CONV_IN_CHUNK)
        a = _dot(hn_ref[...], win_ref[:, a_cols]) + bin_ref[:, a_cols]
        gate = _dot(hn_ref[...], win_ref[:, g_cols]) + bin_ref[:, g_cols]
        u_ref[CONV_HALO:, a_cols] = a * jax.nn.sigmoid(gate)

    def conv(cb):
        _LINK['fresh'] = True
        cols = slice(cb * CONV_LANES, (cb + 1) * CONV_LANES)
        for k0 in range(0, CONV_WIDTH, CONV_TAPS):
            taps = range(k0, min(k0 + CONV_TAPS, CONV_WIDTH))
            ws = {k: dw_ref[k, :, cols] for k in taps}
            loaded = {}

            def src(r):
                if r not in loaded:
                    loaded[r] = u_ref[CONV_HALO + r:CONV_HALO + r + 8, cols]
                return loaded[r]

            for r in range(0, TILE_M, 8):
                acc = start(dwb_ref[:, cols] if k0 == 0 else v_ref[r:r + 8, cols])
                for k in taps:
                    acc = acc + ws[k] * src(r - (CONV_WIDTH - 1 - k) * BATCH)
                v_ref[r:r + 8, cols] = acc
                finished.append(acc)
                if r == 0 and k0 == 0:
                    _LINK['conv'] = acc

    def norm(rb):
        rows = slice(rb * CONV_LN_ROWS, (rb + 1) * CONV_LN_ROWS)
        v = v_ref[rows, :]
        mu = jnp.mean(v, axis=-1, keepdims=True)
        vc = v - mu
        var = jnp.mean(vc * vc, axis=-1, keepdims=True)
        un = vc * lax.rsqrt(var + LN_EPS) * lng_ref[...] + lnb_ref[...]
        un_ref[rows, :] = (un * jax.nn.sigmoid(un)).astype(BF16)

    def post():
        mid_ref[...] = h_ref[...] + _dot(un_ref[...], wout_ref[...])
        u_ref[0:CONV_HALO, :] = u_ref[TILE_M:TILE_M + CONV_HALO, :]

    items = [lambda n=n: glu(n) for n in range(D_MODEL // CONV_IN_CHUNK)]
    items += [lambda cb=cb: conv(cb) for cb in range(D_MODEL // CONV_LANES)]
    items += [lambda rb=rb: norm(rb) for rb in range(TILE_M // CONV_LN_ROWS)]
    return pre, items, post


def _ssm_init(u_ref, x_ref, xs_ref, st_ref, gl_ref):
    st_ref[...] = jnp.zeros(st_ref.shape, F32)


def _ssm_stage(h_ref, g_ref, bmat_ref, cmat_ref, lre_ref, lim_ref, d_ref, wa_ref, wb_ref, mid_ref,
               u_ref, x_ref, xs_ref, st_ref, gl_ref):
    re, im = slice(0, SSM_CHUNK_ST), slice(SSM_CHUNK_ST, 2 * SSM_CHUNK_ST)

    def pre():
        u_ref[...] = _rmsnorm(h_ref[...], g_ref[...])

    def drive(c):
        ch = slice(c * SSM_CHUNK_CH, (c + 1) * SSM_CHUNK_CH)
        x_ref[c % 2] = _dot(u_ref[:, ch].astype(BF16), bmat_ref[c])

    def scan(c):
        lre = jnp.broadcast_to(lre_ref[c], (BATCH, SSM_CHUNK_ST))
        lim = jnp.broadcast_to(lim_ref[c], (BATCH, SSM_CHUNK_ST))
        sre, sim = st_ref[c, :, re], st_ref[c, :, im]
        for t in range(TILE_T):
            rows = slice(t * BATCH, (t + 1) * BATCH)
            nre = lre * sre - lim * sim + x_ref[c % 2, rows, re]
            nim = lre * sim + lim * sre + x_ref[c % 2, rows, im]
            xs_ref[c % 2, rows, re] = nre.astype(BF16)
            xs_ref[c % 2, rows, im] = nim.astype(BF16)
            sre, sim = nre, nim
        st_ref[c, :, re] = sre
        st_ref[c, :, im] = sim

    def readout(c):
        ch = slice(c * SSM_CHUNK_CH, (c + 1) * SSM_CHUNK_CH)
        y = _dot(xs_ref[c % 2], cmat_ref[c]) + d_ref[:, ch] * u_ref[:, ch]
        gl_ref[:, ch] = jax.nn.gelu(y, approximate=True).astype(BF16)

    def post():
        gl = gl_ref[...]
        mid_ref[...] = h_ref[...] + _dot(gl, wa_ref[...]) * jax.nn.sigmoid(_dot(gl, wb_ref[...]))

    items = [lambda: drive(0)]
    for c in range(SSM_CHUNKS):
        if c + 1 < SSM_CHUNKS:
            items.append(lambda c=c: drive(c + 1))
        items.append(lambda c=c: scan(c))
        items.append(lambda c=c: readout(c))
    return pre, items, post


def _ssm_params(lam_re, lam_im, log_dt, b_re, b_im, c_re, c_im):
    lr = jnp.minimum(lam_re, LAM_RE_MAX)
    li = lam_im
    dt = jnp.exp(log_dt)[:, None]
    mag = jnp.exp(lr * dt)
    bar_re, bar_im = mag * jnp.cos(li * dt), mag * jnp.sin(li * dt)
    den = lr * lr + li * li
    q_re = ((bar_re - 1.0) * lr + bar_im * li) / den
    q_im = (bar_im * lr - (bar_re - 1.0) * li) / den
    bb_re = q_re[..., None] * b_re - q_im[..., None] * b_im
    bb_im = q_re[..., None] * b_im + q_im[..., None] * b_re
    eye = jnp.eye(SSM_CHUNK_GROUPS, dtype=F32)

    def in_blocks(w):
        w = w.reshape(SSM_CHUNKS, SSM_CHUNK_GROUPS, SSM_STATE, SSM_GROUP_DIM)
        m = jnp.einsum('cgph,gk->cghkp', w, eye)
        return m.reshape(SSM_CHUNKS, SSM_CHUNK_CH, SSM_CHUNK_ST)

    def out_blocks(w):
        w = w.reshape(SSM_CHUNKS, SSM_CHUNK_GROUPS, SSM_GROUP_DIM, SSM_STATE)
        m = jnp.einsum('cghp,gk->cgpkh', w, eye)
        return m.reshape(SSM_CHUNKS, SSM_CHUNK_ST, SSM_CHUNK_CH)

    bmat = jnp.concatenate([in_blocks(bb_re), in_blocks(bb_im)], axis=2).astype(BF16)
    cmat = jnp.concatenate([out_blocks(c_re), out_blocks(-c_im)], axis=1).astype(BF16)
    lre = bar_re.reshape(SSM_CHUNKS, 1, SSM_CHUNK_ST)
    lim = bar_im.reshape(SSM_CHUNKS, 1, SSM_CHUNK_ST)
    return bmat, cmat, lre, lim


def _layer(h, layer, mixer_stage, mixer_init, mixer_consts, mixer_scratch, ffn_consts, name,
           batch_major_in=False, batch_major_out=False):
    n_mix, n_ffn = len(mixer_consts), len(ffn_consts)
    final = layer == DEPTH - 1

    def body(*refs):
        _LINK.clear()
        h_ref = refs[0]
        mix_c = refs[1:1 + n_mix]
        ffn_c = refs[1 + n_mix:1 + n_mix + n_ffn]
        o_ref = refs[1 + n_mix + n_ffn]
        scratch = list(refs[2 + n_mix + n_ffn:])
        mid_ref, hn_ref, act_ref = scratch[:3]
        del scratch[:3]
        hin_ref, hin_tiles_ref = (scratch.pop(0), scratch.pop(0)) if batch_major_in else (None, None)
        out_ref, out_tiles_ref = (scratch.pop(0), scratch.pop(0)) if batch_major_out else (None, None)
        mix_s = scratch

        @pl.when(pl.program_id(0) == 0)
        def _():
            mid_ref[...] = jnp.zeros((TILE_M, D_MODEL), F32)
            mixer_init(*mix_s)

        if batch_major_in:
            for c in range(D_MODEL // LANES):
                lanes = slice(c * LANES, (c + 1) * LANES)
                for b in range(BATCH):
                    hin_tiles_ref[c, pl.ds(b, TILE_T, stride=BATCH), :] = h_ref[b, :, lanes]
                hin_ref[:, lanes] = hin_tiles_ref[c]
        h2d_ref = hin_ref if batch_major_in else h_ref
        o2d_ref = out_ref if batch_major_out else o_ref
        ffn_pre, ffn_items, ffn_post = _ffn_stage(mid_ref, *ffn_c, o2d_ref, hn_ref, act_ref, final=final)
        mix_pre, mix_items, mix_post = mixer_stage(h2d_ref, *mix_c, mid_ref, *mix_s)
        ffn_pre()
        mix_pre()
        for thunk in _interleave(ffn_items, mix_items):
            thunk()
        ffn_post()
        mix_post()
        if batch_major_out:
            for c in range(D_MODEL // LANES):
                lanes = slice(c * LANES, (c + 1) * LANES)
                out_tiles_ref[c] = out_ref[:, lanes]
                for b in range(BATCH):
                    o_ref[b, :, lanes] = out_tiles_ref[c, pl.ds(b, TILE_T, stride=BATCH), :]

    def stream_spec(batch_major, index):
        if batch_major:
            return pl.BlockSpec((BATCH, TILE_T, D_MODEL), lambda i: (0, index(i), 0))
        return pl.BlockSpec((TILE_M, D_MODEL), lambda i: (index(i), 0))

    relayout_scratch = [pltpu.VMEM((TILE_M, D_MODEL), F32), pltpu.VMEM((D_MODEL // LANES, TILE_M, LANES), F32)]
    relayout_scratch = relayout_scratch * (int(batch_major_in) + int(batch_major_out))
    consts = tuple(mixer_consts) + tuple(ffn_consts)
    stacked = (None,) * n_mix + tuple(layer if c.ndim == 3 else None for c in ffn_consts)
    call = pl.pallas_call(
        body,
        grid=(N_TILES + 1,),
        in_specs=[stream_spec(batch_major_in, lambda i: jnp.minimum(i, N_TILES - 1))]
        + [_const_spec(c, s) for c, s in zip(consts, stacked)],
        out_specs=stream_spec(batch_major_out, lambda i: jnp.maximum(i - 1, 0)),
        out_shape=jax.ShapeDtypeStruct((BATCH, SEQ, D_MODEL) if batch_major_out else (ROWS, D_MODEL), F32),
        scratch_shapes=[pltpu.VMEM((TILE_M, D_MODEL), F32),
                        pltpu.VMEM((TILE_M, D_MODEL), BF16),
                        pltpu.VMEM((TILE_M, D_FF), BF16)]
        + relayout_scratch + list(mixer_scratch),
        compiler_params=pltpu.CompilerParams(
            dimension_semantics=("arbitrary",), vmem_limit_bytes=VMEM_LIMIT),
        name=name,
    )
    return call(h, *consts)


def kernel(x, mix_norm, ffn_norm, w_gate, w_up, w_down, pool_w, pool_b, pool_scale, conv_w_in, conv_b_in,
           conv_dw, conv_dw_b, conv_ln_g, conv_ln_b, conv_w_out, ssm_lam_re, ssm_lam_im, ssm_log_dt,
           ssm_b_re, ssm_b_im, ssm_c_re, ssm_c_im, ssm_d, ssm_w_glu_a, ssm_w_glu_b, final_norm):
    assert x.shape == (BATCH, SEQ, D_MODEL) and x.dtype == F32
    row = lambda v: v.reshape(1, -1)
    h = x
    w_gate, w_up, w_down = w_gate.astype(BF16), w_up.astype(BF16), w_down.astype(BF16)
    for i in range(DEPTH):
        kind, j = i % N_MIXERS, i // N_MIXERS
        g = row(mix_norm[i])
        ffn_consts = (row(ffn_norm[i]), w_gate, w_up, w_down, row(final_norm))
        if kind == 0:
            consts = (g, pool_w[j].astype(BF16), row(pool_b[j]), row(pool_scale[j]))
            scratch = [pltpu.VMEM((POOL_HALO + TILE_M, D_MODEL), F32)]
            h = _layer(h, i, _pool_stage, _pool_init, consts, scratch, ffn_consts, f"layer{i}_pool",
                       batch_major_in=(i == 0), batch_major_out=(i == DEPTH - 1))
        elif kind == 1:
            dw8 = jnp.broadcast_to(conv_dw[j][:, None, :], (CONV_WIDTH, 8, D_MODEL))
            dwb8 = jnp.broadcast_to(row(conv_dw_b[j]), (8, D_MODEL))
            consts = (jnp.zeros((1,), jnp.int32), g, conv_w_in[j].astype(BF16), row(conv_b_in[j]), dw8, dwb8,
                      row(conv_ln_g[j]), row(conv_ln_b[j]), conv_w_out[j].astype(BF16))
            scratch = [pltpu.VMEM((TILE_M, D_MODEL), BF16),
                       pltpu.VMEM((CONV_HALO + TILE_M, D_MODEL), F32),
                       pltpu.VMEM((TILE_M, D_MODEL), F32),
                       pltpu.VMEM((TILE_M, D_MODEL), BF16)]
            h = _layer(h, i, _conv_stage, _conv_init, consts, scratch, ffn_consts, f"layer{i}_conv")
        else:
            bmat, cmat, lre, lim = _ssm_params(ssm_lam_re[j], ssm_lam_im[j], ssm_log_dt[j], ssm_b_re[j],
                                               ssm_b_im[j], ssm_c_re[j], ssm_c_im[j])
            consts = (g, bmat, cmat, lre, lim, row(ssm_d[j]),
                      ssm_w_glu_a[j].astype(BF16), ssm_w_glu_b[j].astype(BF16))
            scratch = [pltpu.VMEM((TILE_M, D_MODEL), F32),
                       pltpu.VMEM((2, TILE_M, 2 * SSM_CHUNK_ST), F32),
                       pltpu.VMEM((2, TILE_M, 2 * SSM_CHUNK_ST), BF16),
                       pltpu.VMEM((SSM_CHUNKS, BATCH, 2 * SSM_CHUNK_ST), F32),
                       pltpu.VMEM((TILE_M, D_MODEL), BF16)]
            h = _layer(h, i, _ssm_stage, _ssm_init, consts, scratch, ffn_consts, f"layer{i}_s5")
    return h
```
